```python
import jax, jax.numpy as jnp
from jax import lax
import numpy as np

D_MODEL = 1024
BATCH = 8
SEQ = 4096
DEPTH = 4

HEAD_DIM = 64
D_MIX = D_MODEL
D_FF = 2752
EPS = 1e-6
NEG_INF = -1e30
BLOCK = 128
CONV_CH = D_MODEL // 2
CONV_WIDTH = 31
SWA_Q_HEADS = (D_MIX - CONV_CH) // HEAD_DIM
SWA_KV_HEADS = 2
SWA_GROUP = SWA_Q_HEADS // SWA_KV_HEADS
SWA_WINDOW = 128
EVEN_IN = 2 * CONV_CH + (SWA_Q_HEADS + 2 * SWA_KV_HEADS) * HEAD_DIM
DIL_HEADS = 8
DIL_PAIRS = ((128, 1), (512, 4), (2048, 16))
POOL_CH = D_MIX - DIL_HEADS * HEAD_DIM
POOL_SIZES = (2, 4, 8, 16)
POOL_GROUP = POOL_CH // len(POOL_SIZES)
ODD_IN = 3 * DIL_HEADS * HEAD_DIM + POOL_CH
N_EVEN = (DEPTH + 1) // 2
N_ODD = DEPTH // 2

kernel_name = "hybrid_conv_swa_dilated_pool_macaron"


def rms_norm(x, g):
    xf = x.astype(jnp.float32)
    y = xf * lax.rsqrt(jnp.mean(xf * xf, axis=-1, keepdims=True) + EPS)
    return (y * g.astype(jnp.float32)).astype(x.dtype)


def layer_norm(x, g, b):
    xf = x.astype(jnp.float32)
    mu = jnp.mean(xf, axis=-1, keepdims=True)
    var = jnp.mean(jnp.square(xf - mu), axis=-1, keepdims=True)
    y = (xf - mu) * lax.rsqrt(var + EPS)
    return (y * g.astype(jnp.float32) + b.astype(jnp.float32)).astype(x.dtype)


def swiglu(x, w_gate, w_up, w_down):
    return (jax.nn.silu(x @ w_gate) * (x @ w_up)) @ w_down


def banded_attention(q, k, v, max_dist, sink=None):
    b, L, kvh, grp, dh = q.shape
    n = -(-L // BLOCK)
    pad = n * BLOCK - L
    qb = jnp.pad(q, ((0, 0), (0, pad), (0, 0), (0, 0), (0, 0))).reshape(b, n, BLOCK, kvh, grp, dh)

    def windows(t):
        tb = jnp.pad(t, ((0, 0), (BLOCK, pad), (0, 0), (0, 0))).reshape(b, n + 1, BLOCK, kvh, dh)
        return jnp.concatenate([tb[:, :-1], tb[:, 1:]], axis=2)

    kw, vw = windows(k), windows(v)
    s = jnp.einsum('bnqhgd,bnkhd->bnhgqk', qb, kw).astype(jnp.float32) * (dh ** -0.5)
    qi = np.arange(BLOCK)[:, None]
    kj = np.arange(2 * BLOCK)[None, :]
    blk = np.arange(n)[:, None, None]
    dist = qi + BLOCK - kj
    valid = (dist >= 0) & (dist <= max_dist) & (blk * BLOCK - BLOCK + kj >= 0)
    s = jnp.where(valid[None, :, None, None], s, NEG_INF)
    m = jnp.max(s, axis=-1)
    if sink is not None:
        m = jnp.maximum(m, sink[None, None, :, :, None])
    p = jnp.exp(s - m[..., None])
    l = jnp.sum(p, axis=-1)
    if sink is not None:
        l = l + jnp.exp(sink[None, None, :, :, None] - m)
    o = jnp.einsum('bnhgqk,bnkhd->bnqhgd', p.astype(v.dtype), vw).astype(jnp.float32)
    m = jnp.moveaxis(m, -1, 2)
    l = jnp.moveaxis(l, -1, 2)
    o = o / l[..., None]

    def to_seq(t):
        return t.reshape(b, n * BLOCK, *t.shape[3:])[:, :L]

    return to_seq(o).astype(q.dtype), to_seq(m), to_seq(l)


def even_mixer(h, w_in, w_out, conv_w, conv_b, ln_g, ln_b, q_g, k_g, sinks):
    b, s, _ = h.shape
    proj = h @ w_in
    cuts = [CONV_CH, 2 * CONV_CH, 2 * CONV_CH + SWA_Q_HEADS * HEAD_DIM,
            2 * CONV_CH + (SWA_Q_HEADS + SWA_KV_HEADS) * HEAD_DIM]
    a_val, a_gate, q, k, v = jnp.split(proj, cuts, axis=-1)
    u = a_val * jax.nn.sigmoid(a_gate)
    u = lax.conv_general_dilated(u, conv_w[:, None, :].astype(u.dtype), (1,), ((CONV_WIDTH - 1, 0),),
                                 dimension_numbers=('NWC', 'WIO', 'NWC'),
                                 feature_group_count=CONV_CH) + conv_b
    u = jax.nn.silu(layer_norm(u, ln_g, ln_b))
    q = rms_norm(q.reshape(b, s, SWA_KV_HEADS, SWA_GROUP, HEAD_DIM), q_g)
    k = rms_norm(k.reshape(b, s, SWA_KV_HEADS, HEAD_DIM), k_g)
    v = v.reshape(b, s, SWA_KV_HEADS, HEAD_DIM)
    o, _, _ = banded_attention(q, k, v, SWA_WINDOW - 1,
                               sinks.reshape(SWA_KV_HEADS, SWA_GROUP).astype(jnp.float32))
    return jnp.concatenate([u, o.reshape(b, s, -1)], axis=-1) @ w_out


def dilated_branch(q, k, v, window, dilation):
    b, s, h, dh = q.shape
    L = s // dilation

    def to_res(t):
        return t.reshape(b, L, dilation, h, dh).transpose(0, 2, 1, 3, 4).reshape(b * dilation, L, h, dh)

    def from_res(t):
        rest = t.shape[2:]
        return t.reshape(b, dilation, L, *rest).swapaxes(1, 2).reshape(b, s, *rest)

    o, m, l = banded_attention(to_res(q)[:, :, :, None], to_res(k), to_res(v), window // dilation)
    return from_res(o[:, :, :, 0]), from_res(m[..., 0]), from_res(l[..., 0])


def multiscale_pool(u, pool_w, pool_scale):
    b, s, _ = u.shape
    uf = u.astype(jnp.float32)
    cs = jnp.pad(jnp.cumsum(uf, axis=1), ((0, 0), (1, 0), (0, 0)))
    outs = []
    for gi, w in enumerate(POOL_SIZES):
        sl = slice(gi * POOL_GROUP, (gi + 1) * POOL_GROUP)
        c = cs[:, :, sl]
        lagged = jnp.pad(c[:, :s + 1 - w], ((0, 0), (w - 1, 0), (0, 0)))
        count = jnp.minimum(jnp.arange(1, s + 1), w).astype(jnp.float32)[:, None]
        pooled = ((c[:, 1:] - lagged) / count - uf[:, :, sl]).astype(u.dtype)
        outs.append(pooled @ pool_w[gi])
    return jnp.concatenate(outs, axis=-1) * pool_scale


def odd_mixer(h, w_in, w_out, q_g, k_g, pool_w, pool_scale):
    b, s, _ = h.shape
    hd = DIL_HEADS * HEAD_DIM
    q, k, v, u = jnp.split(h @ w_in, [hd, 2 * hd, 3 * hd], axis=-1)
    q = rms_norm(q.reshape(b, s, DIL_HEADS, HEAD_DIM), q_g)
    k = rms_norm(k.reshape(b, s, DIL_HEADS, HEAD_DIM), k_g)
    v = v.reshape(b, s, DIL_HEADS, HEAD_DIM)
    outs, ms, ls = zip(*[dilated_branch(q, k, v, w, d) for w, d in DIL_PAIRS])
    ms = jnp.stack(ms)
    wts = jnp.stack(ls) * jnp.exp(ms - jnp.max(ms, axis=0))
    att = jnp.einsum('rbsh,rbshd->bshd', wts, jnp.stack(outs).astype(jnp.float32))
    att = (att / jnp.sum(wts, axis=0)[..., None]).astype(h.dtype)
    pool = multiscale_pool(u, pool_w, pool_scale)
    return jnp.concatenate([att.reshape(b, s, hd), pool], axis=-1) @ w_out


def setup_inputs(seed: int = 0) -> dict:
    key = jax.random.key(seed)
    ks = jax.random.split(key, 24)

    def nrm(k, shape, scale):
        return scale * jax.random.normal(k, shape, jnp.float32)

    return {
        "x": nrm(ks[0], (BATCH, SEQ, D_MODEL), 1.0),
        "norm_g": 1.0 + nrm(ks[1], (DEPTH, 3, D_MODEL), 0.05),
        "ffn_w_gate": nrm(ks[2], (DEPTH, 2, D_MODEL, D_FF), D_MODEL ** -0.5),
        "ffn_w_up": nrm(ks[3], (DEPTH, 2, D_MODEL, D_FF), D_MODEL ** -0.5),
        "ffn_w_down": nrm(ks[4], (DEPTH, 2, D_FF, D_MODEL), D_FF ** -0.5),
        "ev_w_in": nrm(ks[5], (N_EVEN, D_MODEL, EVEN_IN), D_MODEL ** -0.5),
        "ev_w_out": nrm(ks[6], (N_EVEN, D_MIX, D_MODEL), D_MIX ** -0.5),
        "ev_conv_w": nrm(ks[7], (N_EVEN, CONV_WIDTH, CONV_CH), CONV_WIDTH ** -0.5),
        "ev_conv_b": nrm(ks[8], (N_EVEN, CONV_CH), 0.02),
        "ev_ln_g": 1.0 + nrm(ks[9], (N_EVEN, CONV_CH), 0.05),
        "ev_ln_b": nrm(ks[10], (N_EVEN, CONV_CH), 0.02),
        "ev_q_norm_g": 1.0 + nrm(ks[11], (N_EVEN, HEAD_DIM), 0.05),
        "ev_k_norm_g": 1.0 + nrm(ks[12], (N_EVEN, HEAD_DIM), 0.05),
        "ev_sinks": nrm(ks[13], (N_EVEN, SWA_Q_HEADS), 0.5),
        "od_w_in": nrm(ks[14], (N_ODD, D_MODEL, ODD_IN), D_MODEL ** -0.5),
        "od_w_out": nrm(ks[15], (N_ODD, D_MIX, D_MODEL), D_MIX ** -0.5),
        "od_q_norm_g": 1.0 + nrm(ks[16], (N_ODD, HEAD_DIM), 0.05),
        "od_k_norm_g": 1.0 + nrm(ks[17], (N_ODD, HEAD_DIM), 0.05),
        "od_pool_w": nrm(ks[18], (N_ODD, len(POOL_SIZES), POOL_GROUP, POOL_GROUP), POOL_GROUP ** -0.5),
        "od_pool_scale": 1.0 + nrm(ks[19], (N_ODD, POOL_CH), 0.05),
    }


def reference(x, norm_g, ffn_w_gate, ffn_w_up, ffn_w_down,
              ev_w_in, ev_w_out, ev_conv_w, ev_conv_b, ev_ln_g, ev_ln_b,
              ev_q_norm_g, ev_k_norm_g, ev_sinks,
              od_w_in, od_w_out, od_q_norm_g, od_k_norm_g, od_pool_w, od_pool_scale):
    for layer in range(DEPTH):
        g = norm_g[layer]
        x = x + 0.5 * swiglu(rms_norm(x, g[0]), ffn_w_gate[layer, 0], ffn_w_up[layer, 0], ffn_w_down[layer, 0])
        h = rms_norm(x, g[1])
        i = layer // 2
        if layer % 2 == 0:
            mix = even_mixer(h, ev_w_in[i], ev_w_out[i], ev_conv_w[i], ev_conv_b[i], ev_ln_g[i], ev_ln_b[i],
                             ev_q_norm_g[i], ev_k_norm_g[i], ev_sinks[i])
        else:
            mix = odd_mixer(h, od_w_in[i], od_w_out[i], od_q_norm_g[i], od_k_norm_g[i],
                            od_pool_w[i], od_pool_scale[i])
        x = x + mix
        x = x + 0.5 * swiglu(rms_norm(x, g[2]), ffn_w_gate[layer, 1], ffn_w_up[layer, 1], ffn_w_down[layer, 1])
    return x
```

```python
import functools

import jax
import jax.numpy as jnp
import numpy as np
from jax import lax
from jax.experimental import pallas as pl
from jax.experimental.pallas import tpu as pltpu

F32 = jnp.float32
BF16 = jnp.bfloat16

D_MODEL = 1024
BATCH = 8
SEQ = 4096
DEPTH = 4
HEAD_DIM = 64
D_FF = 2752
EPS = 1e-6
NEG_INF = -1e30
BLOCK = 128
CONV_CH = 512
CONV_WIDTH = 31
SWA_Q_HEADS = 8
SWA_WINDOW = 128
EVEN_IN = 1792
DIL_HEADS = 8
POOL_CH = 512
POOL_SIZES = (2, 4, 8, 16)
POOL_GROUP = 128
ODD_IN = 2048
QK_SCALE = HEAD_DIM ** -0.5

LANES = 128
SUBLANES = 8
MXU_DIM = 256
D_FF_PAD = -(-D_FF // MXU_DIM) * MXU_DIM
N_RES = 16
RES_ROWS = SEQ // N_RES
TM = 512
EV_CHUNK = 1024
CONV_HALO = 32
VMEM_LIMIT = 56 * 1024 * 1024


def _params(n_axes):
    return pltpu.CompilerParams(dimension_semantics=("arbitrary",) * n_axes,
                                vmem_limit_bytes=VMEM_LIMIT)


def _const_spec(shape):
    nd = len(shape)
    return pl.BlockSpec(shape, lambda *_: (0,) * nd, pipeline_mode=pl.Buffered(1))


def _rms(x, g):
    return x * lax.rsqrt(jnp.mean(x * x, axis=-1, keepdims=True) + EPS) * g


def _sigmoid(x):
    return 1.0 / (1.0 + jnp.exp(-x))


def _dot(a, b):
    return jnp.dot(a, b, preferred_element_type=F32)


def _dot_nt(a, b):
    return lax.dot_general(a, b, (((1,), (1,)), ((), ())), preferred_element_type=F32)


def _head_sumsq(x, bd_ref):
    sq = x * x
    hi = sq.astype(BF16)
    lo = (sq - hi.astype(F32)).astype(BF16)
    bd = bd_ref[...]
    outs = []
    for c in range(x.shape[1] // MXU_DIM):
        sl = slice(c * MXU_DIM, (c + 1) * MXU_DIM)
        outs.append(_dot(hi[:, sl], bd) + _dot(lo[:, sl], bd))
    return outs[0] if len(outs) == 1 else jnp.concatenate(outs, axis=-1)


def _head_rms(x, ss, g):
    return x * lax.rsqrt(ss * (1.0 / HEAD_DIM) + EPS) * g


def _load_x(x_ref, res16):
    if res16:
        return jnp.concatenate([x_ref[0, :, :D_MODEL], x_ref[0, :, D_MODEL:]], axis=0)
    return x_ref[...]


def _store_x(o_ref, val, res16):
    if res16:
        o_ref[0, :, :D_MODEL] = val[:RES_ROWS]
        o_ref[0, :, D_MODEL:] = val[RES_ROWS:]
    else:
        o_ref[...] = val


def _load_feat(f_ref, res16):
    if res16:
        return f_ref[0].reshape(2 * RES_ROWS, f_ref.shape[-1])
    return f_ref[...]


def _ffn_kernel(*refs, has_proj, res16):
    if has_proj:
        x_ref, fa_ref, fb_ref, woa_ref, wob_ref, g_ref, wg_ref, wu_ref, wd_ref, o_ref = refs
    else:
        x_ref, g_ref, wg_ref, wu_ref, wd_ref, o_ref = refs
    x = _load_x(x_ref, res16)
    if has_proj:
        x = x + _dot(_load_feat(fa_ref, res16), woa_ref[...]) \
              + _dot(_load_feat(fb_ref, res16), wob_ref[...])
    xn = _rms(x, g_ref[...]).astype(BF16)
    h = _dot(xn, wg_ref[...])
    u = _dot(xn, wu_ref[...])
    a = (h * _sigmoid(h) * u).astype(BF16)
    y = _dot(a, wd_ref[...])
    _store_x(o_ref, x + 0.5 * y, res16)


def _x_spec(res16):
    if res16:
        return pl.BlockSpec((1, RES_ROWS, 2 * D_MODEL), lambda b, j: (b, 0, j))
    return pl.BlockSpec((TM, D_MODEL), lambda i: (i, 0))


def _feat_spec(res16, width):
    if res16:
        return pl.BlockSpec((1, 2, RES_ROWS, width), lambda b, j: (b, j, 0, 0))
    return pl.BlockSpec((TM, width), lambda i: (i, 0))


def _ffn_call(x, g, wg, wu, wd, proj=None, res16=False):
    grid = (BATCH, N_RES // 2) if res16 else (BATCH * SEQ // TM,)
    in_specs = [_x_spec(res16)]
    args = [x]
    if proj is not None:
        fa, fb, woa, wob = proj
        in_specs += [_feat_spec(res16, fa.shape[-1]), _feat_spec(res16, fb.shape[-1]),
                     _const_spec(woa.shape), _const_spec(wob.shape)]
        args += [fa, fb, woa, wob]
    in_specs += [_const_spec(g.shape), _const_spec(wg.shape), _const_spec(wu.shape),
                 _const_spec(wd.shape)]
    args += [g, wg, wu, wd]
    return pl.pallas_call(
        functools.partial(_ffn_kernel, has_proj=proj is not None, res16=res16),
        grid=grid, in_specs=in_specs, out_specs=_x_spec(res16),
        out_shape=jax.ShapeDtypeStruct(x.shape, F32),
        compiler_params=_params(len(grid)), name="ffn_proj" if proj is not None else "ffn",
    )(*args)


def _even_in_kernel(x_ref, g_ref, w_ref, bd_ref, gq_ref, gk_ref, u_ref, q_ref, k_ref, v_ref):
    hn = _rms(x_ref[...], g_ref[...]).astype(BF16)
    proj = _dot(hn, w_ref[...])
    a_val = proj[:, :CONV_CH]
    a_gate = proj[:, CONV_CH:2 * CONV_CH]
    u_ref[...] = a_val * _sigmoid(a_gate)
    qkv = proj[:, 2 * CONV_CH:]
    ss = _head_sumsq(qkv, bd_ref)
    q = qkv[:, :512]
    k = qkv[:, 512:640]
    q_ref[...] = (_head_rms(q, ss[:, :512], gq_ref[...]) * QK_SCALE).astype(BF16)
    k_ref[...] = _head_rms(k, ss[:, 512:640], gk_ref[...]).astype(BF16)
    v_ref[...] = qkv[:, 640:].astype(BF16)


def _even_in_call(x, g, w_in, bd, gq, gk):
    n = x.shape[0]
    row = lambda width: pl.BlockSpec((TM, width), lambda i: (i, 0))
    return pl.pallas_call(
        _even_in_kernel, grid=(n // TM,),
        in_specs=[row(D_MODEL), _const_spec(g.shape), _const_spec(w_in.shape),
                  _const_spec(bd.shape), _const_spec(gq.shape), _const_spec(gk.shape)],
        out_specs=[row(512), row(512), row(128), row(128)],
        out_shape=[jax.ShapeDtypeStruct((n, 512), F32), jax.ShapeDtypeStruct((n, 512), BF16),
                   jax.ShapeDtypeStruct((n, 128), BF16), jax.ShapeDtypeStruct((n, 128), BF16)],
        compiler_params=_params(1), name="even_in",
    )(x, g, w_in, bd, gq, gk)


def _pair_softmax_pv(s, bias, kv_v, sink=None):
    nh = s.shape[0] // BLOCK
    s3 = s.reshape(nh, BLOCK, 2 * BLOCK) + bias[None]
    m = jnp.max(s3, axis=-1, keepdims=True)
    if sink is not None:
        m = jnp.maximum(m, sink)
    p = jnp.exp(s3 - m)
    l = jnp.sum(p, axis=-1, keepdims=True)
    if sink is not None:
        l = l + jnp.exp(sink - m)
    lo = lax.broadcasted_iota(jnp.int32, (2 * BLOCK, LANES), 1) < HEAD_DIM
    zero = jnp.zeros_like(kv_v)
    vbd = jnp.concatenate([jnp.where(lo, kv_v, zero), jnp.where(lo, zero, kv_v)], axis=0)
    lo_q = lax.broadcasted_iota(jnp.int32, (BLOCK, LANES), 1) < HEAD_DIM
    outs = []
    for pr in range(nh // 2):
        pcat = jnp.concatenate([p[2 * pr], p[2 * pr + 1]], axis=-1).astype(BF16)
        acc = _dot(pcat, vbd)
        m_pair = jnp.where(lo_q, m[2 * pr], m[2 * pr + 1])
        l_pair = jnp.where(lo_q, l[2 * pr], l[2 * pr + 1])
        outs.append((acc, m_pair, l_pair))
    return outs


def _stack_pairs(q_pairs):
    lo = lax.broadcasted_iota(jnp.int32, (BLOCK, LANES), 1) < HEAD_DIM
    parts = []
    for q in q_pairs:
        zero = jnp.zeros_like(q)
        parts += [jnp.where(lo, q, zero), jnp.where(lo, zero, q)]
    return jnp.concatenate(parts, axis=0)


def _even_mix_kernel(u_ref, uh_ref, q_ref, k_ref, kh_ref, v_ref, vh_ref, cw_ref, cb_ref,
                     lng_ref, lnb_ref, sink_ref, bias_ref, fa_ref, fb_ref,
                     ucat, kcat, vcat):
    c = pl.program_id(1)
    first = c == 0

    ucat[:CONV_HALO] = jnp.where(first, 0.0, uh_ref[0])
    ucat[CONV_HALO:CONV_HALO + EV_CHUNK] = u_ref[0]
    ucat[CONV_HALO + EV_CHUNK:] = jnp.zeros((SUBLANES, CONV_CH), F32)
    off = CONV_HALO - (CONV_WIDTH - 1)

    def conv_rows(i, carry):
        base = pl.multiple_of(i * BLOCK, BLOCK)
        wins = [ucat[pl.ds(base + SUBLANES * eh, BLOCK + SUBLANES), :]
                for eh in range(CONV_HALO // SUBLANES + 1)]
        acc = jnp.broadcast_to(cb_ref[...], (BLOCK, CONV_CH))
        for el in range(SUBLANES):
            part = None
            for eh, win in enumerate(wins):
                j = SUBLANES * eh + el - off
                if 0 <= j < CONV_WIDTH:
                    term = cw_ref[j:j + 1, :] * win
                    part = term if part is None else part + term
            acc = acc + part[el:el + BLOCK]
        mu = jnp.mean(acc, axis=-1, keepdims=True)
        d = acc - mu
        var = jnp.mean(d * d, axis=-1, keepdims=True)
        y = d * lax.rsqrt(var + EPS) * lng_ref[...] + lnb_ref[...]
        fa_ref[0, pl.ds(base, BLOCK), :] = (y * _sigmoid(y)).astype(BF16)
        return carry

    lax.fori_loop(0, EV_CHUNK // BLOCK, conv_rows, 0)

    kcat[:BLOCK] = kh_ref[0]
    kcat[BLOCK:] = k_ref[0]
    vcat[:BLOCK] = vh_ref[0]
    vcat[BLOCK:] = v_ref[0]
    sink = sink_ref[...]

    def attn_block(i, carry):
        base = pl.multiple_of(i * BLOCK, BLOCK)
        kk = kcat[pl.ds(base, 2 * BLOCK), :]
        vv = vcat[pl.ds(base, 2 * BLOCK), :]
        qs = _stack_pairs([q_ref[0, pl.ds(base, BLOCK), pr * LANES:(pr + 1) * LANES]
                           for pr in range(SWA_Q_HEADS // 2)])
        bias = bias_ref[jnp.where(jnp.logical_and(first, i == 0), 1, 0)]
        s = _dot_nt(qs, kk)
        for pr, (acc, _, l_pair) in enumerate(_pair_softmax_pv(s, bias, vv, sink)):
            fb_ref[0, pl.ds(base, BLOCK), pr * LANES:(pr + 1) * LANES] = (acc / l_pair).astype(BF16)
        return carry

    lax.fori_loop(0, EV_CHUNK // BLOCK, attn_block, 0)


def _even_mix_call(u, q, k, v, cw, cb, lng, lnb, sink, bias):
    nchunk = SEQ // EV_CHUNK
    cur = lambda width: pl.BlockSpec((1, EV_CHUNK, width), lambda b, c: (b, c, 0))
    halo = lambda rows, width: pl.BlockSpec(
        (1, rows, width), lambda b, c: (b, jnp.maximum(c * (EV_CHUNK // rows) - 1, 0), 0))
    return pl.pallas_call(
        _even_mix_kernel, grid=(BATCH, nchunk),
        in_specs=[cur(512), halo(CONV_HALO, 512), cur(512), cur(128), halo(BLOCK, 128),
                  cur(128), halo(BLOCK, 128), _const_spec(cw.shape), _const_spec(cb.shape),
                  _const_spec(lng.shape), _const_spec(lnb.shape), _const_spec(sink.shape),
                  _const_spec(bias.shape)],
        out_specs=[cur(512), cur(512)],
        out_shape=[jax.ShapeDtypeStruct((BATCH, SEQ, 512), BF16)] * 2,
        scratch_shapes=[pltpu.VMEM((CONV_HALO + EV_CHUNK + SUBLANES, CONV_CH), F32),
                        pltpu.VMEM((BLOCK + EV_CHUNK, LANES), BF16),
                        pltpu.VMEM((BLOCK + EV_CHUNK, LANES), BF16)],
        compiler_params=_params(2), name="even_mix",
    )(u, u, q, k, k, v, v, cw, cb, lng, lnb, sink, bias)


def _odd_in_kernel(x_ref, g_ref, w_ref, bd_ref, gq_ref, gk_ref, q_ref, k_ref, v_ref, u_ref):
    hn = _rms(_load_x(x_ref, True), g_ref[...]).astype(BF16)
    proj = _dot(hn, w_ref[...])
    qk = proj[:, :1024]
    ss = _head_sumsq(qk, bd_ref)
    q = (_head_rms(qk[:, :512], ss[:, :512], gq_ref[...]) * QK_SCALE).astype(BF16)
    k = _head_rms(qk[:, 512:], ss[:, 512:], gk_ref[...]).astype(BF16)
    v = proj[:, 1024:1536].astype(BF16)
    u = proj[:, 1536:]
    for ref, val in ((q_ref, q), (k_ref, k), (v_ref, v), (u_ref, u)):
        ref[0, 0] = val[:RES_ROWS]
        ref[0, 1] = val[RES_ROWS:]


def _odd_in_call(x16, g, w_in, bd, gq, gk):
    out = lambda: pl.BlockSpec((1, 2, RES_ROWS, 512), lambda b, j: (b, j, 0, 0))
    shp = lambda dt: jax.ShapeDtypeStruct((BATCH, N_RES, RES_ROWS, 512), dt)
    return pl.pallas_call(
        _odd_in_kernel, grid=(BATCH, N_RES // 2),
        in_specs=[_x_spec(True), _const_spec(g.shape), _const_spec(w_in.shape),
                  _const_spec(bd.shape), _const_spec(gq.shape), _const_spec(gk.shape)],
        out_specs=[out(), out(), out(), out()],
        out_shape=[shp(BF16), shp(BF16), shp(BF16), shp(F32)],
        compiler_params=_params(2), name="odd_in",
    )(x16, g, w_in, bd, gq, gk)


def _dil_attn_kernel(q_ref, k_ref, v_ref, bias_ref, o_ref, qf, kf, vf, m_s, l_s, acc_s):
    qf[...] = q_ref[0].astype(F32)
    kf[...] = k_ref[0].astype(F32)
    vf[...] = v_ref[0].astype(F32)

    def block(q, kprev, kcur, vprev, vcur, bias):
        kk = jnp.concatenate([kprev, kcur], axis=0)
        vv = jnp.concatenate([vprev, vcur], axis=0)
        s = _dot_nt(_stack_pairs([q]), kk)
        return _pair_softmax_pv(s, bias, vv)[0]

    def merge(idx, acc, m, l):
        m_old = m_s[idx]
        m_new = jnp.maximum(m_old, m)
        a = jnp.exp(m_old - m_new)
        b = jnp.exp(m - m_new)
        m_s[idx] = m_new
        l_s[idx] = a * l_s[idx] + b * l
        acc_s[idx] = a * acc_s[idx] + b * acc

    def branch16(r, carry):
        for n in range(RES_ROWS // BLOCK):
            cur = slice(n * BLOCK, (n + 1) * BLOCK)
            prev = slice(max(n - 1, 0) * BLOCK, max(n - 1, 0) * BLOCK + BLOCK)
            acc, m, l = block(q_ref[0, r, cur, :], k_ref[0, r, prev, :], k_ref[0, r, cur, :],
                              v_ref[0, r, prev, :], v_ref[0, r, cur, :],
                              bias_ref[0, 1 if n == 0 else 0])
            m_s[r, cur, :] = m
            l_s[r, cur, :] = l
            acc_s[r, cur, :] = acc
        return carry

    lax.fori_loop(0, N_RES, branch16, 0)

    def strided_branch(bi, nslab, src_q, src_k, src_v, cast):
        rows = BLOCK // nslab
        nblk = RES_ROWS // rows

        def body(it, carry):
            r0 = it // nblk
            n = it % nblk
            cur = pl.multiple_of(n * rows, rows)
            prev = pl.multiple_of(jnp.maximum(n - 1, 0) * rows, rows)
            slabs = [r0 + (N_RES // nslab) * c for c in range(nslab)]

            def gather(src, start):
                return cast(jnp.concatenate([src(s, pl.ds(start, rows)) for s in slabs], axis=0))

            bias = bias_ref[bi, jnp.where(n == 0, 1, 0)]
            acc, m, l = block(gather(src_q, cur), gather(src_k, prev), gather(src_k, cur),
                              gather(src_v, prev), gather(src_v, cur), bias)
            for c, s in enumerate(slabs):
                part = slice(c * rows, (c + 1) * rows)
                merge((s, pl.ds(cur, rows), slice(None)), acc[part], m[part], l[part])
            return carry

        lax.fori_loop(0, (N_RES // nslab) * nblk, body, 0)

    strided_branch(1, 4, lambda s, rs: q_ref[0, s, rs, :], lambda s, rs: k_ref[0, s, rs, :],
                   lambda s, rs: v_ref[0, s, rs, :], lambda x: x)
    strided_branch(2, 16, lambda s, rs: qf[s, rs, :], lambda s, rs: kf[s, rs, :],
                   lambda s, rs: vf[s, rs, :], lambda x: x.astype(BF16))

    o_ref[0] = (acc_s[...] / l_s[...]).astype(BF16)


def _dil_attn_call(q, k, v, bias):
    blk = lambda: pl.BlockSpec((1, N_RES, RES_ROWS, LANES), lambda b, p: (b, 0, 0, p))
    scr = lambda: pltpu.VMEM((N_RES, RES_ROWS, LANES), F32)
    return pl.pallas_call(
        _dil_attn_kernel, grid=(BATCH, DIL_HEADS // 2),
        in_specs=[blk(), blk(), blk(), _const_spec(bias.shape)],
        out_specs=blk(),
        out_shape=jax.ShapeDtypeStruct((BATCH, N_RES, RES_ROWS, 512), BF16),
        scratch_shapes=[scr() for _ in range(6)],
        compiler_params=_params(2), name="dil_attn",
    )(q, k, v, bias)


def _pool_kernel(u_ref, w_ref, sc_ref, o_ref):
    row0 = lax.broadcasted_iota(jnp.int32, (RES_ROWS, POOL_GROUP), 0) == 0

    def shift(x):
        return jnp.where(row0, 0.0, pltpu.roll(x, 1, axis=0))

    for gi, w in enumerate(POOL_SIZES):
        lanes = slice(gi * POOL_GROUP, (gi + 1) * POOL_GROUP)
        slab = [u_ref[0, r, :, lanes] for r in range(N_RES)]
        pre = [slab[0]]
        for r in range(1, N_RES):
            pre.append(pre[-1] + slab[r])
        tot_sh = shift(pre[-1])
        pooled = []
        for r in range(N_RES):
            if r >= w:
                win = pre[r] - pre[r - w]
            elif r == w - 1:
                win = pre[r]
            else:
                win = pre[r] + (tot_sh - shift(pre[r - w + N_RES]))
            inv = jnp.where(row0, 1.0 / min(r + 1, w), 1.0 / w)
            pooled.append((win * inv - slab[r]).astype(BF16))
        out = _dot(jnp.concatenate(pooled, axis=0), w_ref[gi]) * sc_ref[:, lanes]
        for r in range(N_RES):
            o_ref[0, r, :, lanes] = out[r * RES_ROWS:(r + 1) * RES_ROWS].astype(BF16)


def _pool_call(u, w, sc):
    blk = lambda: pl.BlockSpec((1, N_RES, RES_ROWS, POOL_CH), lambda b: (b, 0, 0, 0))
    return pl.pallas_call(
        _pool_kernel, grid=(BATCH,),
        in_specs=[blk(), _const_spec(w.shape), _const_spec(sc.shape)],
        out_specs=blk(),
        out_shape=jax.ShapeDtypeStruct((BATCH, N_RES, RES_ROWS, POOL_CH), BF16),
        compiler_params=_params(1), name="pool",
    )(u, w, sc)


def _band_bias(nslab, max_dist):
    rows = BLOCK // nslab
    s = np.arange(BLOCK)
    pos = nslab * (s % rows) + s // rows
    dist = pos[:, None] - np.concatenate([pos - BLOCK, pos])[None, :]
    gen = np.where((dist >= 0) & (dist <= max_dist), 0.0, NEG_INF).astype(np.float32)
    fst = gen.copy()
    fst[:, :BLOCK] = NEG_INF
    return np.stack([gen, fst])


def _block_diag_ones():
    i = np.arange(MXU_DIM) // HEAD_DIM
    return jnp.asarray((i[:, None] == i[None, :]).astype(np.float32), dtype=BF16)


def _pad_ff(w, axis):
    pad = [(0, 0), (0, 0)]
    pad[axis] = (0, D_FF_PAD - D_FF)
    return jnp.pad(w, pad).astype(BF16)


def _tile_heads(g, n):
    return jnp.tile(g, n)[None, :]


_EV_HEAD_ORDER = np.array([h for p in range(4) for h in (p, 4 + p)])
_EV_Q_PERM = (_EV_HEAD_ORDER[:, None] * HEAD_DIM + np.arange(HEAD_DIM)[None, :]).reshape(-1)


def kernel(x, norm_g, ffn_w_gate, ffn_w_up, ffn_w_down, ev_w_in, ev_w_out, ev_conv_w, ev_conv_b,
           ev_ln_g, ev_ln_b, ev_q_norm_g, ev_k_norm_g, ev_sinks, od_w_in, od_w_out, od_q_norm_g,
           od_k_norm_g, od_pool_w, od_pool_scale):
    bd = _block_diag_ones()
    ev_bias = jnp.asarray(_band_bias(1, SWA_WINDOW - 1))
    od_bias = jnp.asarray(np.stack([_band_bias(1, BLOCK), _band_bias(4, BLOCK),
                                    _band_bias(16, BLOCK)]))
    x = x.reshape(BATCH * SEQ, D_MODEL)
    for layer in range(DEPTH):
        g = norm_g[layer]
        i = layer // 2
        ffn_w = [(g[2 * j:2 * j + 1], _pad_ff(ffn_w_gate[layer, j], 1),
                  _pad_ff(ffn_w_up[layer, j], 1), _pad_ff(ffn_w_down[layer, j], 0))
                 for j in range(2)]
        x = _ffn_call(x, *ffn_w[0])
        if layer % 2 == 0:
            w_in = jnp.concatenate([ev_w_in[i][:, :1024], ev_w_in[i][:, 1024 + _EV_Q_PERM],
                                    ev_w_in[i][:, 1536:]], axis=1).astype(BF16)
            u, q, k, v = _even_in_call(x, g[1:2], w_in, bd,
                                       _tile_heads(ev_q_norm_g[i], 8), _tile_heads(ev_k_norm_g[i], 2))
            sink = ev_sinks[i][_EV_HEAD_ORDER].reshape(SWA_Q_HEADS, 1, 1)
            r3 = lambda t: t.reshape(BATCH, SEQ, t.shape[-1])
            cw = jnp.pad(ev_conv_w[i], ((0, 1), (0, 0)))
            fa, fb = _even_mix_call(r3(u), r3(q), r3(k), r3(v), cw, ev_conv_b[i][None],
                                    ev_ln_g[i][None], ev_ln_b[i][None], sink, ev_bias)
            w_out = ev_w_out[i].astype(BF16)
            proj = (fa.reshape(-1, 512), fb.reshape(-1, 512), w_out[:512], w_out[512 + _EV_Q_PERM])
            x = _ffn_call(x, *ffn_w[1], proj=proj)
        else:
            x16 = x.reshape(BATCH, RES_ROWS, N_RES * D_MODEL)
            q, k, v, u = _odd_in_call(x16, g[1:2], od_w_in[i].astype(BF16), bd,
                                      _tile_heads(od_q_norm_g[i], 8), _tile_heads(od_k_norm_g[i], 8))
            att = _dil_attn_call(q, k, v, od_bias)
            pool = _pool_call(u, od_pool_w[i].astype(BF16), od_pool_scale[i][None])
            w_out = od_w_out[i].astype(BF16)
            x16 = _ffn_call(x16, *ffn_w[1], proj=(att, pool, w_out[:512], w_out[512:]), res16=True)
            x = x16.reshape(BATCH * SEQ, D_MODEL)
    return x.reshape(BATCH, SEQ, D_MODEL)
```

```python
import functools

import jax
import jax.numpy as jnp
import numpy as np
from jax import lax
from jax.experimental import pallas as pl
from jax.experimental.pallas import tpu as pltpu

F32 = jnp.float32
BF16 = jnp.bfloat16

D_MODEL = 1024
BATCH = 8
SEQ = 4096
DEPTH = 4
HEAD_DIM = 64
D_FF = 2752
EPS = 1e-6
NEG_INF = -1e30
BLOCK = 128
CONV_CH = 512
CONV_WIDTH = 31
SWA_Q_HEADS = 8
SWA_WINDOW = 128
EVEN_IN = 1792
DIL_HEADS = 8
POOL_CH = 512
POOL_SIZES = (2, 4, 8, 16)
POOL_GROUP = 128
ODD_IN = 2048
QK_SCALE = HEAD_DIM ** -0.5

LANES = 128
SUBLANES = 8
MXU_DIM = 256
D_FF_PAD = -(-D_FF // MXU_DIM) * MXU_DIM
N_RES = 16
RES_ROWS = SEQ // N_RES
TM = 512
EV_CHUNK = 1024
CONV_HALO = 32
DIL_STEP_PAIRS = 2
QK_BOUND = HEAD_DIM * QK_SCALE * 1.03
MAX_CONST_SHIFT = 20.0
MAX_SINK_OVER_SHIFT = 60.0
VMEM_LIMIT = 56 * 1024 * 1024


def _params(n_axes):
    return pltpu.CompilerParams(dimension_semantics=("arbitrary",) * n_axes,
                                vmem_limit_bytes=VMEM_LIMIT)


def _const_spec(shape):
    nd = len(shape)
    return pl.BlockSpec(shape, lambda *_: (0,) * nd, pipeline_mode=pl.Buffered(1))


def _rms(x, g):
    return x * lax.rsqrt(jnp.mean(x * x, axis=-1, keepdims=True) + EPS) * g


def _sigmoid(x):
    return 1.0 / (1.0 + jnp.exp(-x))


def _dot(a, b):
    return jnp.dot(a, b, preferred_element_type=F32)


def _dot_nt(a, b):
    return lax.dot_general(a, b, (((1,), (1,)), ((), ())), preferred_element_type=F32)


def _head_sumsq(x, bd_ref):
    sq = x * x
    hi = sq.astype(BF16)
    lo = (sq - hi.astype(F32)).astype(BF16)
    bd = bd_ref[...]
    outs = []
    for c in range(x.shape[1] // MXU_DIM):
        sl = slice(c * MXU_DIM, (c + 1) * MXU_DIM)
        outs.append(_dot(hi[:, sl], bd) + _dot(lo[:, sl], bd))
    return outs[0] if len(outs) == 1 else jnp.concatenate(outs, axis=-1)


def _head_rms(x, ss, g):
    return x * lax.rsqrt(ss * (1.0 / HEAD_DIM) + EPS) * g


def _load_x(x_ref, res16):
    if res16:
        return jnp.concatenate([x_ref[0, :, :D_MODEL], x_ref[0, :, D_MODEL:]], axis=0)
    return x_ref[...]


def _store_x(o_ref, val, res16):
    if res16:
        o_ref[0, :, :D_MODEL] = val[:RES_ROWS]
        o_ref[0, :, D_MODEL:] = val[RES_ROWS:]
    else:
        o_ref[...] = val


def _load_feat(f_ref, res16):
    if res16:
        return f_ref[0].reshape(2 * RES_ROWS, f_ref.shape[-1])
    return f_ref[...]


def _ffn_kernel(*refs, has_proj, res16):
    if has_proj:
        x_ref, fa_ref, fb_ref, woa_ref, wob_ref, g_ref, wg_ref, wu_ref, wd_ref, o_ref = refs
    else:
        x_ref, g_ref, wg_ref, wu_ref, wd_ref, o_ref = refs
    x = _load_x(x_ref, res16)
    if has_proj:
        x = x + _dot(_load_feat(fa_ref, res16), woa_ref[...]) \
              + _dot(_load_feat(fb_ref, res16), wob_ref[...])
    xn = _rms(x, g_ref[...]).astype(BF16)
    h = _dot(xn, wg_ref[...])
    u = _dot(xn, wu_ref[...])
    a = (h * _sigmoid(h) * u).astype(BF16)
    y = _dot(a, wd_ref[...])
    _store_x(o_ref, x + 0.5 * y, res16)


def _x_spec(res16):
    if res16:
        return pl.BlockSpec((1, RES_ROWS, 2 * D_MODEL), lambda b, j: (b, 0, j))
    return pl.BlockSpec((TM, D_MODEL), lambda i: (i, 0))


def _feat_spec(res16, width):
    if res16:
        return pl.BlockSpec((1, 2, RES_ROWS, width), lambda b, j: (b, j, 0, 0))
    return pl.BlockSpec((TM, width), lambda i: (i, 0))


def _ffn_call(x, g, wg, wu, wd, proj=None, res16=False):
    grid = (BATCH, N_RES // 2) if res16 else (BATCH * SEQ // TM,)
    in_specs = [_x_spec(res16)]
    args = [x]
    if proj is not None:
        fa, fb, woa, wob = proj
        in_specs += [_feat_spec(res16, fa.shape[-1]), _feat_spec(res16, fb.shape[-1]),
                     _const_spec(woa.shape), _const_spec(wob.shape)]
        args += [fa, fb, woa, wob]
    in_specs += [_const_spec(g.shape), _const_spec(wg.shape), _const_spec(wu.shape),
                 _const_spec(wd.shape)]
    args += [g, wg, wu, wd]
    return pl.pallas_call(
        functools.partial(_ffn_kernel, has_proj=proj is not None, res16=res16),
        grid=grid, in_specs=in_specs, out_specs=_x_spec(res16),
        out_shape=jax.ShapeDtypeStruct(x.shape, F32),
        compiler_params=_params(len(grid)), name="ffn_proj" if proj is not None else "ffn",
    )(*args)


def _even_in_kernel(x_ref, g_ref, w_ref, bd_ref, gq_ref, gk_ref, u_ref, q_ref, k_ref, v_ref):
    hn = _rms(x_ref[...], g_ref[...]).astype(BF16)
    proj = _dot(hn, w_ref[...])
    a_val = proj[:, :CONV_CH]
    a_gate = proj[:, CONV_CH:2 * CONV_CH]
    u_ref[...] = a_val * _sigmoid(a_gate)
    qkv = proj[:, 2 * CONV_CH:]
    ss = _head_sumsq(qkv, bd_ref)
    q = qkv[:, :512]
    k = qkv[:, 512:640]
    q_ref[...] = (_head_rms(q, ss[:, :512], gq_ref[...]) * QK_SCALE).astype(BF16)
    k_ref[...] = _head_rms(k, ss[:, 512:640], gk_ref[...]).astype(BF16)
    v_ref[...] = qkv[:, 640:].astype(BF16)


def _even_in_call(x, g, w_in, bd, gq, gk):
    n = x.shape[0]
    row = lambda width: pl.BlockSpec((TM, width), lambda i: (i, 0))
    return pl.pallas_call(
        _even_in_kernel, grid=(n // TM,),
        in_specs=[row(D_MODEL), _const_spec(g.shape), _const_spec(w_in.shape),
                  _const_spec(bd.shape), _const_spec(gq.shape), _const_spec(gk.shape)],
        out_specs=[row(512), row(512), row(128), row(128)],
        out_shape=[jax.ShapeDtypeStruct((n, 512), F32), jax.ShapeDtypeStruct((n, 512), BF16),
                   jax.ShapeDtypeStruct((n, 128), BF16), jax.ShapeDtypeStruct((n, 128), BF16)],
        compiler_params=_params(1), name="even_in",
    )(x, g, w_in, bd, gq, gk)


def _pv_operand(vv):
    lo = lax.broadcasted_iota(jnp.int32, vv.shape, 1) < HEAD_DIM
    zero = jnp.zeros_like(vv)
    ones_lo = jnp.where(lo, 1.0, 0.0)
    top = jnp.concatenate([jnp.where(lo, vv, zero), ones_lo.astype(BF16)], axis=-1)
    bot = jnp.concatenate([jnp.where(lo, zero, vv), (1.0 - ones_lo).astype(BF16)], axis=-1)
    return jnp.concatenate([top, bot], axis=0)


def _attend(s, bias, vv, exact, sink=None):
    nh = s.shape[0] // BLOCK
    s3 = s.reshape(nh, BLOCK, s.shape[1]) + bias[None]
    lo_q = lax.broadcasted_iota(jnp.int32, (BLOCK, LANES), 1) < HEAD_DIM
    m = None
    if exact:
        m = jnp.max(s3, axis=-1, keepdims=True)
        if sink is not None:
            m = jnp.maximum(m, sink)
        s3 = s3 - m
        if sink is not None:
            sink = jnp.exp(sink - m)
    p = jnp.exp(s3)
    vext = _pv_operand(vv)
    outs = []
    for pr in range(nh // 2):
        pcat = jnp.concatenate([p[2 * pr], p[2 * pr + 1]], axis=-1).astype(BF16)
        res = _dot(pcat, vext)
        if sink is not None:
            term = jnp.where(lo_q, sink[2 * pr], sink[2 * pr + 1]) if exact else sink[pr]
            res = jnp.concatenate([res[:, :LANES], res[:, LANES:] + term], axis=-1)
        m_pair = jnp.where(lo_q, m[2 * pr], m[2 * pr + 1]) if exact else None
        outs.append((res, m_pair))
    return outs


def _stack_pairs(q_pairs):
    lo = lax.broadcasted_iota(jnp.int32, (BLOCK, LANES), 1) < HEAD_DIM
    parts = []
    for q in q_pairs:
        zero = jnp.zeros_like(q)
        parts += [jnp.where(lo, q, zero), jnp.where(lo, zero, q)]
    return jnp.concatenate(parts, axis=0)


def _even_mix_kernel(u_ref, uh_ref, q_ref, k_ref, kh_ref, v_ref, vh_ref, cw_ref, cb_ref,
                     lng_ref, lnb_ref, sink_ref, bias_ref, fa_ref, fb_ref,
                     ucat, kcat, vcat, *, exact):
    c = pl.program_id(1)
    first = c == 0

    ucat[:CONV_HALO] = jnp.where(first, 0.0, uh_ref[0])
    ucat[CONV_HALO:CONV_HALO + EV_CHUNK] = u_ref[0]
    ucat[CONV_HALO + EV_CHUNK:] = jnp.zeros((SUBLANES, CONV_CH), F32)
    off = CONV_HALO - (CONV_WIDTH - 1)

    def conv_rows(i, carry):
        base = pl.multiple_of(i * BLOCK, BLOCK)
        wins = [ucat[pl.ds(base + SUBLANES * eh, BLOCK + SUBLANES), :]
                for eh in range(CONV_HALO // SUBLANES + 1)]
        acc = jnp.broadcast_to(cb_ref[...], (BLOCK, CONV_CH))
        for el in range(SUBLANES):
            part = None
            for eh, win in enumerate(wins):
                j = SUBLANES * eh + el - off
                if 0 <= j < CONV_WIDTH:
                    term = cw_ref[j:j + 1, :] * win
                    part = term if part is None else part + term
            acc = acc + part[el:el + BLOCK]
        mu = jnp.mean(acc, axis=-1, keepdims=True)
        d = acc - mu
        var = jnp.mean(d * d, axis=-1, keepdims=True)
        y = d * lax.rsqrt(var + EPS) * lng_ref[...] + lnb_ref[...]
        fa_ref[0, pl.ds(base, BLOCK), :] = (y * _sigmoid(y)).astype(BF16)
        return carry

    lax.fori_loop(0, EV_CHUNK // BLOCK, conv_rows, 0)

    kcat[:BLOCK] = kh_ref[0]
    kcat[BLOCK:] = k_ref[0]
    vcat[:BLOCK] = vh_ref[0]
    vcat[BLOCK:] = v_ref[0]
    sink = sink_ref[...]

    def attn_block(i, carry):
        base = pl.multiple_of(i * BLOCK, BLOCK)
        kk = kcat[pl.ds(base, 2 * BLOCK), :]
        vv = vcat[pl.ds(base, 2 * BLOCK), :]
        qs = _stack_pairs([q_ref[0, pl.ds(base, BLOCK), pr * LANES:(pr + 1) * LANES]
                           for pr in range(SWA_Q_HEADS // 2)])
        bias = bias_ref[jnp.where(jnp.logical_and(first, i == 0), 1, 0)]
        s = _dot_nt(qs, kk)
        for pr, (res, _) in enumerate(_attend(s, bias, vv, exact, sink)):
            out = res[:, :LANES] / res[:, LANES:]
            fb_ref[0, pl.ds(base, BLOCK), pr * LANES:(pr + 1) * LANES] = out.astype(BF16)
        return carry

    lax.fori_loop(0, EV_CHUNK // BLOCK, attn_block, 0)


def _even_mix_call(u, q, k, v, cw, cb, lng, lnb, sink, bias, exact):
    nchunk = SEQ // EV_CHUNK
    cur = lambda width: pl.BlockSpec((1, EV_CHUNK, width), lambda b, c: (b, c, 0))
    halo = lambda rows, width: pl.BlockSpec(
        (1, rows, width), lambda b, c: (b, jnp.maximum(c * (EV_CHUNK // rows) - 1, 0), 0))
    return pl.pallas_call(
        functools.partial(_even_mix_kernel, exact=exact), grid=(BATCH, nchunk),
        in_specs=[cur(512), halo(CONV_HALO, 512), cur(512), cur(128), halo(BLOCK, 128),
                  cur(128), halo(BLOCK, 128), _const_spec(cw.shape), _const_spec(cb.shape),
                  _const_spec(lng.shape), _const_spec(lnb.shape), _const_spec(sink.shape),
                  _const_spec(bias.shape)],
        out_specs=[cur(512), cur(512)],
        out_shape=[jax.ShapeDtypeStruct((BATCH, SEQ, 512), BF16)] * 2,
        scratch_shapes=[pltpu.VMEM((CONV_HALO + EV_CHUNK + SUBLANES, CONV_CH), F32),
                        pltpu.VMEM((BLOCK + EV_CHUNK, LANES), BF16),
                        pltpu.VMEM((BLOCK + EV_CHUNK, LANES), BF16)],
        compiler_params=_params(2), name="even_mix_exact" if exact else "even_mix",
    )(u, u, q, k, k, v, v, cw, cb, lng, lnb, sink, bias)


def _odd_in_kernel(x_ref, g_ref, w_ref, bd_ref, gq_ref, gk_ref, q_ref, k_ref, v_ref, u_ref):
    hn = _rms(_load_x(x_ref, True), g_ref[...]).astype(BF16)
    proj = _dot(hn, w_ref[...])
    qk = proj[:, :1024]
    ss = _head_sumsq(qk, bd_ref)
    q = _head_rms(qk[:, :512], ss[:, :512], gq_ref[...]) * QK_SCALE
    k = _head_rms(qk[:, 512:], ss[:, 512:], gk_ref[...])
    v = proj[:, 1024:1536]
    u = proj[:, 1536:]
    for ref, val in ((q_ref, q), (k_ref, k), (v_ref, v), (u_ref, u)):
        ref[0, 0] = val[:RES_ROWS]
        ref[0, 1] = val[RES_ROWS:]


def _odd_in_call(x16, g, w_in, bd, gq, gk):
    out = lambda: pl.BlockSpec((1, 2, RES_ROWS, 512), lambda b, j: (b, j, 0, 0))
    shp = lambda dt: jax.ShapeDtypeStruct((BATCH, N_RES, RES_ROWS, 512), dt)
    return pl.pallas_call(
        _odd_in_kernel, grid=(BATCH, N_RES // 2),
        in_specs=[_x_spec(True), _const_spec(g.shape), _const_spec(w_in.shape),
                  _const_spec(bd.shape), _const_spec(gq.shape), _const_spec(gk.shape)],
        out_specs=[out(), out(), out(), out()],
        out_shape=[shp(F32), shp(F32), shp(F32), shp(F32)],
        compiler_params=_params(2), name="odd_in",
    )(x16, g, w_in, bd, gq, gk)


def _dil_attn_kernel(q_ref, k_ref, v_ref, bias_ref, o_ref, st, *m_scratch, exact):
    m_s = m_scratch[0] if exact else None

    def lanes(pi):
        return slice(pi * LANES, (pi + 1) * LANES)

    def gather(ref, slabs, start, rows, pi):
        parts = [ref[0, s, pl.ds(start, rows), lanes(pi)] for s in slabs]
        return jnp.concatenate(parts, axis=0).astype(BF16)

    def block(q, kk, vv, bias):
        return _attend(_dot_nt(_stack_pairs([q]), kk), bias, vv, exact)[0]

    def put(pi, slab, rows, res, m, init):
        idx = (pi, slab, rows, slice(None))
        if init:
            st[idx] = res
            if exact:
                m_s[idx] = m
        elif not exact:
            st[idx] += res
        else:
            m_old = m_s[idx]
            m_new = jnp.maximum(m_old, m)
            a = jnp.exp(m_old - m_new)
            b = jnp.exp(m - m_new)
            m_s[idx] = m_new
            st[idx] = (jnp.concatenate([a, a], axis=-1) * st[idx]
                       + jnp.concatenate([b, b], axis=-1) * res)

    def put_chunks(pi, slabs, start, rows, res, m):
        for c, s in enumerate(slabs):
            part = slice(c * rows, (c + 1) * rows)
            put(pi, s, pl.ds(start, rows), res[part], m[part] if exact else None, False)

    def branch16(i, carry):
        bias = bias_ref[0, 0]
        for r in (2 * i, 2 * i + 1):
            for pi in range(DIL_STEP_PAIRS):
                q, k, v = (ref[0, r, :, lanes(pi)].astype(BF16) for ref in (q_ref, k_ref, v_ref))
                res, m = block(q[:BLOCK], k[:BLOCK], v[:BLOCK], bias[:, BLOCK:])
                put(pi, r, slice(0, BLOCK), res, m, True)
                res, m = block(q[BLOCK:], k, v, bias)
                put(pi, r, slice(BLOCK, 2 * BLOCK), res, m, True)
        return carry

    lax.fori_loop(0, N_RES // 2, branch16, 0)

    def strided_block(bi, nslab, r0, n, first_possible):
        rows = BLOCK // nslab
        cur = pl.multiple_of(n * rows, rows)
        prev = pl.multiple_of(jnp.maximum(n - 1, 0) * rows, rows)
        slabs = [r0 + (N_RES // nslab) * c for c in range(nslab)]
        bias = bias_ref[bi, jnp.where(n == 0, 1, 0)] if first_possible else bias_ref[bi, 0]
        for pi in range(DIL_STEP_PAIRS):
            kk = jnp.concatenate([gather(k_ref, slabs, prev, rows, pi),
                                  gather(k_ref, slabs, cur, rows, pi)], axis=0)
            vv = jnp.concatenate([gather(v_ref, slabs, prev, rows, pi),
                                  gather(v_ref, slabs, cur, rows, pi)], axis=0)
            res, m = block(gather(q_ref, slabs, cur, rows, pi), kk, vv, bias)
            put_chunks(pi, slabs, cur, rows, res, m)

    def branch4(n, carry):
        for r0 in range(4):
            strided_block(1, 4, r0, n, True)
        return carry

    lax.fori_loop(0, RES_ROWS // (BLOCK // 4), branch4, 0)

    def branch1(i, carry):
        for dn in range(4):
            strided_block(2, N_RES, 0, 4 * i + dn, dn == 0)
        return carry

    lax.fori_loop(0, RES_ROWS // (BLOCK // N_RES) // 4, branch1, 0)

    for pi in range(DIL_STEP_PAIRS):
        o_ref[0, :, :, lanes(pi)] = (st[pi, :, :, :LANES] / st[pi, :, :, LANES:]).astype(BF16)


def _dil_attn_call(q, k, v, bias, exact):
    width = DIL_STEP_PAIRS * LANES
    blk = lambda: pl.BlockSpec((1, N_RES, RES_ROWS, width), lambda b, p: (b, 0, 0, p))
    scratch = [pltpu.VMEM((DIL_STEP_PAIRS, N_RES, RES_ROWS, 2 * LANES), F32)]
    if exact:
        scratch.append(pltpu.VMEM((DIL_STEP_PAIRS, N_RES, RES_ROWS, LANES), F32))
    return pl.pallas_call(
        functools.partial(_dil_attn_kernel, exact=exact),
        grid=(BATCH, DIL_HEADS // 2 // DIL_STEP_PAIRS),
        in_specs=[blk(), blk(), blk(), _const_spec(bias.shape)],
        out_specs=blk(),
        out_shape=jax.ShapeDtypeStruct((BATCH, N_RES, RES_ROWS, 512), BF16),
        scratch_shapes=scratch,
        compiler_params=_params(2), name="dil_attn_exact" if exact else "dil_attn",
    )(q, k, v, bias)


def _pool_kernel(u_ref, w_ref, sc_ref, o_ref):
    row0 = lax.broadcasted_iota(jnp.int32, (RES_ROWS, POOL_GROUP), 0) == 0

    def shift(x):
        return jnp.where(row0, 0.0, pltpu.roll(x, 1, axis=0))

    for gi, w in enumerate(POOL_SIZES):
        lanes = slice(gi * POOL_GROUP, (gi + 1) * POOL_GROUP)
        slab = [u_ref[0, r, :, lanes] for r in range(N_RES)]
        pre = [slab[0]]
        for r in range(1, N_RES):
            pre.append(pre[-1] + slab[r])
        tot_sh = shift(pre[-1])
        pooled = []
        for r in range(N_RES):
            if r >= w:
                win = pre[r] - pre[r - w]
            elif r == w - 1:
                win = pre[r]
            else:
                win = pre[r] + (tot_sh - shift(pre[r - w + N_RES]))
            inv = jnp.where(row0, 1.0 / min(r + 1, w), 1.0 / w)
            pooled.append((win * inv - slab[r]).astype(BF16))
        out = _dot(jnp.concatenate(pooled, axis=0), w_ref[gi]) * sc_ref[:, lanes]
        for r in range(N_RES):
            o_ref[0, r, :, lanes] = out[r * RES_ROWS:(r + 1) * RES_ROWS].astype(BF16)


def _pool_call(u, w, sc):
    blk = lambda: pl.BlockSpec((1, N_RES, RES_ROWS, POOL_CH), lambda b: (b, 0, 0, 0))
    return pl.pallas_call(
        _pool_kernel, grid=(BATCH,),
        in_specs=[blk(), _const_spec(w.shape), _const_spec(sc.shape)],
        out_specs=blk(),
        out_shape=jax.ShapeDtypeStruct((BATCH, N_RES, RES_ROWS, POOL_CH), BF16),
        compiler_params=_params(1), name="pool",
    )(u, w, sc)


def _band_bias(nslab, max_dist):
    rows = BLOCK // nslab
    s = np.arange(BLOCK)
    pos = nslab * (s % rows) + s // rows
    dist = pos[:, None] - np.concatenate([pos - BLOCK, pos])[None, :]
    gen = np.where((dist >= 0) & (dist <= max_dist), 0.0, NEG_INF).astype(np.float32)
    fst = gen.copy()
    fst[:, :BLOCK] = NEG_INF
    return np.stack([gen, fst])


def _block_diag_ones():
    i = np.arange(MXU_DIM) // HEAD_DIM
    return jnp.asarray((i[:, None] == i[None, :]).astype(np.float32), dtype=BF16)


def _pad_ff(w, axis):
    pad = [(0, 0), (0, 0)]
    pad[axis] = (0, D_FF_PAD - D_FF)
    return jnp.pad(w, pad).astype(BF16)


def _tile_heads(g, n):
    return jnp.tile(g, n)[None, :]


def _qk_shift(gq, gk):
    return QK_BOUND * jnp.max(jnp.abs(gq)) * jnp.max(jnp.abs(gk))


_EV_HEAD_ORDER = np.array([h for p in range(4) for h in (p, 4 + p)])
_EV_Q_PERM = (_EV_HEAD_ORDER[:, None] * HEAD_DIM + np.arange(HEAD_DIM)[None, :]).reshape(-1)


def kernel(x, norm_g, ffn_w_gate, ffn_w_up, ffn_w_down, ev_w_in, ev_w_out, ev_conv_w, ev_conv_b,
           ev_ln_g, ev_ln_b, ev_q_norm_g, ev_k_norm_g, ev_sinks, od_w_in, od_w_out, od_q_norm_g,
           od_k_norm_g, od_pool_w, od_pool_scale):
    bd = _block_diag_ones()
    ev_bias = jnp.asarray(_band_bias(1, SWA_WINDOW - 1))
    od_bias = jnp.asarray(np.stack([_band_bias(1, BLOCK), _band_bias(4, BLOCK),
                                    _band_bias(16, BLOCK)]))
    x = x.reshape(BATCH * SEQ, D_MODEL)
    for layer in range(DEPTH):
        g = norm_g[layer]
        i = layer // 2
        ffn_w = [(g[2 * j:2 * j + 1], _pad_ff(ffn_w_gate[layer, j], 1),
                  _pad_ff(ffn_w_up[layer, j], 1), _pad_ff(ffn_w_down[layer, j], 0))
                 for j in range(2)]
        x = _ffn_call(x, *ffn_w[0])
        if layer % 2 == 0:
            w_in = jnp.concatenate([ev_w_in[i][:, :1024], ev_w_in[i][:, 1024 + _EV_Q_PERM],
                                    ev_w_in[i][:, 1536:]], axis=1).astype(BF16)
            u, q, k, v = _even_in_call(x, g[1:2], w_in, bd,
                                       _tile_heads(ev_q_norm_g[i], 8), _tile_heads(ev_k_norm_g[i], 2))
            sink = ev_sinks[i][_EV_HEAD_ORDER]
            r3 = lambda t: t.reshape(BATCH, SEQ, t.shape[-1])
            cw = jnp.pad(ev_conv_w[i], ((0, 1), (0, 0)))
            shift = _qk_shift(ev_q_norm_g[i], ev_k_norm_g[i])
            use_shift = jnp.logical_and(shift <= MAX_CONST_SHIFT,
                                        jnp.max(sink) - shift <= MAX_SINK_OVER_SHIFT)
            sink_terms = jnp.repeat(jnp.exp(sink - shift), HEAD_DIM).reshape(SWA_Q_HEADS // 2, 1, LANES)
            mix = lambda sk, bs, exact: _even_mix_call(
                r3(u), r3(q), r3(k), r3(v), cw, ev_conv_b[i][None], ev_ln_g[i][None],
                ev_ln_b[i][None], sk, bs, exact)
            fa, fb = lax.cond(use_shift,
                              lambda: mix(sink_terms, ev_bias - shift, False),
                              lambda: mix(sink.reshape(SWA_Q_HEADS, 1, 1), ev_bias, True))
            w_out = ev_w_out[i].astype(BF16)
            proj = (fa.reshape(-1, 512), fb.reshape(-1, 512), w_out[:512], w_out[512 + _EV_Q_PERM])
            x = _ffn_call(x, *ffn_w[1], proj=proj)
        else:
            x16 = x.reshape(BATCH, RES_ROWS, N_RES * D_MODEL)
            q, k, v, u = _odd_in_call(x16, g[1:2], od_w_in[i].astype(BF16), bd,
                                      _tile_heads(od_q_norm_g[i], 8), _tile_heads(od_k_norm_g[i], 8))
            shift = _qk_shift(od_q_norm_g[i], od_k_norm_g[i])
            att = lax.cond(shift <= MAX_CONST_SHIFT,
                           lambda: _dil_attn_call(q, k, v, od_bias - shift, False),
                           lambda: _dil_attn_call(q, k, v, od_bias, True))
            pool = _pool_call(u, od_pool_w[i].astype(BF16), od_pool_scale[i][None])
            w_out = od_w_out[i].astype(BF16)
            x16 = _ffn_call(x16, *ffn_w[1], proj=(att, pool, w_out[:512], w_out[512:]), res16=True)
            x = x16.reshape(BATCH * SEQ, D_MODEL)
    return x.reshape(BATCH, SEQ, D_MODEL)
```

```python
import functools

import jax
import jax.numpy as jnp
import numpy as np
from jax import lax
from jax.experimental import pallas as pl
from jax.experimental.pallas import tpu as pltpu

F32 = jnp.float32
BF16 = jnp.bfloat16

D_MODEL = 1024
BATCH = 8
SEQ = 4096
DEPTH = 4
HEAD_DIM = 64
D_FF = 2752
EPS = 1e-6
NEG_INF = -1e30
BLOCK = 128
CONV_CH = 512
CONV_WIDTH = 31
SWA_Q_HEADS = 8
SWA_WINDOW = 128
EVEN_IN = 1792
DIL_HEADS = 8
POOL_CH = 512
POOL_SIZES = (2, 4, 8, 16)
POOL_GROUP = 128
ODD_IN = 2048
QK_SCALE = HEAD_DIM ** -0.5

LANES = 128
SUBLANES = 8
MXU_DIM = 256
D_FF_PAD = -(-D_FF // MXU_DIM) * MXU_DIM
N_RES = 16
RES_ROWS = SEQ // N_RES
TM = 512
EV_CHUNK = 1024
CONV_HALO = 32
DIL_STEP_PAIRS = 2
DIL16_SLABS_PER_TRIP = 4
DIL4_BLOCKS_PER_TRIP = 2
DIL1_BLOCKS_PER_TRIP = 8
QK_BOUND = HEAD_DIM * QK_SCALE * 1.03
MAX_CONST_SHIFT = 20.0
MAX_SINK_OVER_SHIFT = 60.0
VMEM_LIMIT = 56 * 1024 * 1024


def _params(n_axes):
    return pltpu.CompilerParams(dimension_semantics=("arbitrary",) * n_axes,
                                vmem_limit_bytes=VMEM_LIMIT)


def _const_spec(shape):
    nd = len(shape)
    return pl.BlockSpec(shape, lambda *_: (0,) * nd, pipeline_mode=pl.Buffered(1))


def _rms(x, g):
    return x * lax.rsqrt(jnp.mean(x * x, axis=-1, keepdims=True) + EPS) * g


def _sigmoid(x):
    return 1.0 / (1.0 + jnp.exp(-x))


def _dot(a, b):
    return jnp.dot(a, b, preferred_element_type=F32)


def _dot_nt(a, b):
    return lax.dot_general(a, b, (((1,), (1,)), ((), ())), preferred_element_type=F32)


def _head_sumsq(x, bd_ref):
    sq = (x * x).astype(BF16)
    bd = bd_ref[...]
    outs = [_dot(sq[:, c * MXU_DIM:(c + 1) * MXU_DIM], bd) for c in range(x.shape[1] // MXU_DIM)]
    return outs[0] if len(outs) == 1 else jnp.concatenate(outs, axis=-1)


def _head_rms(x, ss, g):
    return x * lax.rsqrt(ss * (1.0 / HEAD_DIM) + EPS) * g


def _permute_rows(perm_ref, val):
    return _dot(perm_ref[...], val).astype(BF16)


def _ffn_kernel(*refs, has_proj, res16):
    perm_ref = None
    if has_proj and res16:
        x_ref, fa_ref, fb_ref, perm_ref, woa_ref, wob_ref, g_ref, wg_ref, wu_ref, wd_ref, o_ref = refs
    elif has_proj:
        x_ref, fa_ref, fb_ref, woa_ref, wob_ref, g_ref, wg_ref, wu_ref, wd_ref, o_ref = refs
    else:
        x_ref, g_ref, wg_ref, wu_ref, wd_ref, o_ref = refs
    x = x_ref[...]
    if has_proj:
        feats = []
        for f_ref in (fa_ref, fb_ref):
            if res16:
                feats.append(_permute_rows(perm_ref, f_ref[0].reshape(TM, f_ref.shape[-1])))
            else:
                feats.append(f_ref[...])
        x = x + _dot(feats[0], woa_ref[...]) + _dot(feats[1], wob_ref[...])
    xn = _rms(x, g_ref[...]).astype(BF16)
    h = _dot(xn, wg_ref[...])
    u = _dot(xn, wu_ref[...])
    a = (h * _sigmoid(h) * u).astype(BF16)
    y = _dot(a, wd_ref[...])
    o_ref[...] = x + 0.5 * y


def _row_spec(width):
    return pl.BlockSpec((TM, width), lambda i: (i, 0))


def _res16_spec(width):
    tiles = SEQ // TM
    return pl.BlockSpec((1, N_RES, TM // N_RES, width), lambda i: (i // tiles, 0, i % tiles, 0))


def _ffn_call(x, g, wg, wu, wd, proj=None, perm=None):
    res16 = perm is not None
    in_specs = [_row_spec(D_MODEL)]
    args = [x]
    if proj is not None:
        fa, fb, woa, wob = proj
        fspec = _res16_spec if res16 else _row_spec
        in_specs += [fspec(fa.shape[-1]), fspec(fb.shape[-1])]
        args += [fa, fb]
        if res16:
            in_specs.append(_const_spec(perm.shape))
            args.append(perm)
        in_specs += [_const_spec(woa.shape), _const_spec(wob.shape)]
        args += [woa, wob]
    in_specs += [_const_spec(g.shape), _const_spec(wg.shape), _const_spec(wu.shape),
                 _const_spec(wd.shape)]
    args += [g, wg, wu, wd]
    return pl.pallas_call(
        functools.partial(_ffn_kernel, has_proj=proj is not None, res16=res16),
        grid=(BATCH * SEQ // TM,), in_specs=in_specs, out_specs=_row_spec(D_MODEL),
        out_shape=jax.ShapeDtypeStruct(x.shape, F32),
        compiler_params=_params(1), name="ffn_proj" if proj is not None else "ffn",
    )(*args)


def _even_in_kernel(x_ref, g_ref, w_ref, bd_ref, gq_ref, gk_ref, u_ref, q_ref, k_ref, v_ref):
    hn = _rms(x_ref[...], g_ref[...]).astype(BF16)
    proj = _dot(hn, w_ref[...])
    a_val = proj[:, :CONV_CH]
    a_gate = proj[:, CONV_CH:2 * CONV_CH]
    u_ref[...] = a_val * _sigmoid(a_gate)
    qkv = proj[:, 2 * CONV_CH:]
    ss = _head_sumsq(qkv, bd_ref)
    q = qkv[:, :512]
    k = qkv[:, 512:640]
    q_ref[...] = (_head_rms(q, ss[:, :512], gq_ref[...]) * QK_SCALE).astype(BF16)
    k_ref[...] = _head_rms(k, ss[:, 512:640], gk_ref[...]).astype(BF16)
    v_ref[...] = qkv[:, 640:].astype(BF16)


def _even_in_call(x, g, w_in, bd, gq, gk):
    n = x.shape[0]
    row = lambda width: pl.BlockSpec((TM, width), lambda i: (i, 0))
    return pl.pallas_call(
        _even_in_kernel, grid=(n // TM,),
        in_specs=[row(D_MODEL), _const_spec(g.shape), _const_spec(w_in.shape),
                  _const_spec(bd.shape), _const_spec(gq.shape), _const_spec(gk.shape)],
        out_specs=[row(512), row(512), row(128), row(128)],
        out_shape=[jax.ShapeDtypeStruct((n, 512), F32), jax.ShapeDtypeStruct((n, 512), BF16),
                   jax.ShapeDtypeStruct((n, 128), BF16), jax.ShapeDtypeStruct((n, 128), BF16)],
        compiler_params=_params(1), name="even_in",
    )(x, g, w_in, bd, gq, gk)


def _pv_operand(vv):
    lo = lax.broadcasted_iota(jnp.int32, vv.shape, 1) < HEAD_DIM
    zero = jnp.zeros_like(vv)
    ones_lo = jnp.where(lo, 1.0, 0.0)
    top = jnp.concatenate([jnp.where(lo, vv, zero), ones_lo.astype(BF16)], axis=-1)
    bot = jnp.concatenate([jnp.where(lo, zero, vv), (1.0 - ones_lo).astype(BF16)], axis=-1)
    return jnp.concatenate([top, bot], axis=0)


def _attend(s, bias, vv, exact, sink=None):
    nh = s.shape[0] // BLOCK
    s3 = s.reshape(nh, BLOCK, s.shape[1]) + bias[None]
    lo_q = lax.broadcasted_iota(jnp.int32, (BLOCK, LANES), 1) < HEAD_DIM
    m = None
    if exact:
        m = jnp.max(s3, axis=-1, keepdims=True)
        if sink is not None:
            m = jnp.maximum(m, sink)
        s3 = s3 - m
        if sink is not None:
            sink = jnp.exp(sink - m)
    p = jnp.exp(s3)
    vext = _pv_operand(vv)
    outs = []
    for pr in range(nh // 2):
        pcat = jnp.concatenate([p[2 * pr], p[2 * pr + 1]], axis=-1).astype(BF16)
        res = _dot(pcat, vext)
        if sink is not None:
            term = jnp.where(lo_q, sink[2 * pr], sink[2 * pr + 1]) if exact else sink[pr]
            res = jnp.concatenate([res[:, :LANES], res[:, LANES:] + term], axis=-1)
        m_pair = jnp.where(lo_q, m[2 * pr], m[2 * pr + 1]) if exact else None
        outs.append((res, m_pair))
    return outs


def _stack_pairs(q_pairs):
    lo = lax.broadcasted_iota(jnp.int32, (BLOCK, LANES), 1) < HEAD_DIM
    parts = []
    for q in q_pairs:
        zero = jnp.zeros_like(q)
        parts += [jnp.where(lo, q, zero), jnp.where(lo, zero, q)]
    return jnp.concatenate(parts, axis=0)


def _even_mix_kernel(u_ref, uh_ref, q_ref, k_ref, kh_ref, v_ref, vh_ref, cw_ref, cb_ref,
                     lng_ref, lnb_ref, sink_ref, bias_ref, fa_ref, fb_ref,
                     ucat, kcat, vcat, *, exact):
    c = pl.program_id(1)
    first = c == 0

    ucat[:CONV_HALO] = jnp.where(first, 0.0, uh_ref[0])
    ucat[CONV_HALO:CONV_HALO + EV_CHUNK] = u_ref[0]
    ucat[CONV_HALO + EV_CHUNK:] = jnp.zeros((SUBLANES, CONV_CH), F32)
    off = CONV_HALO - (CONV_WIDTH - 1)

    def conv_rows(i, carry):
        base = pl.multiple_of(i * BLOCK, BLOCK)
        wins = [ucat[pl.ds(base + SUBLANES * eh, BLOCK + SUBLANES), :]
                for eh in range(CONV_HALO // SUBLANES + 1)]
        acc = jnp.broadcast_to(cb_ref[...], (BLOCK, CONV_CH))
        for el in range(SUBLANES):
            part = None
            for eh, win in enumerate(wins):
                j = SUBLANES * eh + el - off
                if 0 <= j < CONV_WIDTH:
                    term = cw_ref[j:j + 1, :] * win
                    part = term if part is None else part + term
            acc = acc + part[el:el + BLOCK]
        mu = jnp.mean(acc, axis=-1, keepdims=True)
        d = acc - mu
        var = jnp.mean(d * d, axis=-1, keepdims=True)
        y = d * lax.rsqrt(var + EPS) * lng_ref[...] + lnb_ref[...]
        fa_ref[0, pl.ds(base, BLOCK), :] = (y * _sigmoid(y)).astype(BF16)
        return carry

    kcat[:BLOCK] = kh_ref[0]
    kcat[BLOCK:] = k_ref[0]
    vcat[:BLOCK] = vh_ref[0]
    vcat[BLOCK:] = v_ref[0]
    sink = sink_ref[...]

    def attn_block(i, carry):
        base = pl.multiple_of(i * BLOCK, BLOCK)
        kk = kcat[pl.ds(base, 2 * BLOCK), :]
        vv = vcat[pl.ds(base, 2 * BLOCK), :]
        qs = _stack_pairs([q_ref[0, pl.ds(base, BLOCK), pr * LANES:(pr + 1) * LANES]
                           for pr in range(SWA_Q_HEADS // 2)])
        bias = bias_ref[jnp.where(jnp.logical_and(first, i == 0), 1, 0)]
        s = _dot_nt(qs, kk)
        for pr, (res, _) in enumerate(_attend(s, bias, vv, exact, sink)):
            out = res[:, :LANES] / res[:, LANES:]
            fb_ref[0, pl.ds(base, BLOCK), pr * LANES:(pr + 1) * LANES] = out.astype(BF16)
        return carry

    def both(i, carry):
        return attn_block(i, conv_rows(i, carry))

    lax.fori_loop(0, EV_CHUNK // BLOCK, both, 0)


def _even_mix_call(u, q, k, v, cw, cb, lng, lnb, sink, bias, exact):
    nchunk = SEQ // EV_CHUNK
    cur = lambda width: pl.BlockSpec((1, EV_CHUNK, width), lambda b, c: (b, c, 0))
    halo = lambda rows, width: pl.BlockSpec(
        (1, rows, width), lambda b, c: (b, jnp.maximum(c * (EV_CHUNK // rows) - 1, 0), 0))
    return pl.pallas_call(
        functools.partial(_even_mix_kernel, exact=exact), grid=(BATCH, nchunk),
        in_specs=[cur(512), halo(CONV_HALO, 512), cur(512), cur(128), halo(BLOCK, 128),
                  cur(128), halo(BLOCK, 128), _const_spec(cw.shape), _const_spec(cb.shape),
                  _const_spec(lng.shape), _const_spec(lnb.shape), _const_spec(sink.shape),
                  _const_spec(bias.shape)],
        out_specs=[cur(512), cur(512)],
        out_shape=[jax.ShapeDtypeStruct((BATCH, SEQ, 512), BF16)] * 2,
        scratch_shapes=[pltpu.VMEM((CONV_HALO + EV_CHUNK + SUBLANES, CONV_CH), F32),
                        pltpu.VMEM((BLOCK + EV_CHUNK, LANES), BF16),
                        pltpu.VMEM((BLOCK + EV_CHUNK, LANES), BF16)],
        compiler_params=_params(2), name="even_mix_exact" if exact else "even_mix",
    )(u, u, q, k, k, v, v, cw, cb, lng, lnb, sink, bias)


def _odd_in_kernel(x_ref, g_ref, perm_ref, w_ref, bd_ref, gq_ref, gk_ref,
                   q_ref, k_ref, v_ref, u_ref):
    hn = _rms(x_ref[...], g_ref[...]).astype(BF16)
    hn = _permute_rows(perm_ref, hn)
    proj = _dot(hn, w_ref[...])
    qk = proj[:, :1024]
    ss = _head_sumsq(qk, bd_ref)
    q = _head_rms(qk[:, :512], ss[:, :512], gq_ref[...]) * QK_SCALE
    k = _head_rms(qk[:, 512:], ss[:, 512:], gk_ref[...])
    v = proj[:, 1024:1536]
    u = proj[:, 1536:]
    rows = TM // N_RES
    for ref, val in ((q_ref, q), (k_ref, k), (v_ref, v), (u_ref, u)):
        for r in range(N_RES):
            ref[0, r] = val[r * rows:(r + 1) * rows]


def _odd_in_call(x, g, perm, w_in, bd, gq, gk):
    shp = jax.ShapeDtypeStruct((BATCH, N_RES, RES_ROWS, 512), F32)
    return pl.pallas_call(
        _odd_in_kernel, grid=(BATCH * SEQ // TM,),
        in_specs=[_row_spec(D_MODEL), _const_spec(g.shape), _const_spec(perm.shape),
                  _const_spec(w_in.shape), _const_spec(bd.shape), _const_spec(gq.shape),
                  _const_spec(gk.shape)],
        out_specs=[_res16_spec(512)] * 4,
        out_shape=[shp] * 4,
        compiler_params=_params(1), name="odd_in",
    )(x, g, perm, w_in, bd, gq, gk)


def _dil_attn_kernel(q_ref, k_ref, v_ref, bias_ref, o_ref, st, *m_scratch, exact):
    m_s = m_scratch[0] if exact else None

    def lanes(pi):
        return slice(pi * LANES, (pi + 1) * LANES)

    def gather(ref, slabs, start, rows, pi):
        parts = [ref[0, s, pl.ds(start, rows), lanes(pi)] for s in slabs]
        return jnp.concatenate(parts, axis=0).astype(BF16)

    def block(q, kk, vv, bias):
        return _attend(_dot_nt(_stack_pairs([q]), kk), bias, vv, exact)[0]

    def put(pi, slab, rows, res, m, init):
        idx = (pi, slab, rows, slice(None))
        if init:
            st[idx] = res
            if exact:
                m_s[idx] = m
        elif not exact:
            st[idx] += res
        else:
            m_old = m_s[idx]
            m_new = jnp.maximum(m_old, m)
            a = jnp.exp(m_old - m_new)
            b = jnp.exp(m - m_new)
            m_s[idx] = m_new
            st[idx] = (jnp.concatenate([a, a], axis=-1) * st[idx]
                       + jnp.concatenate([b, b], axis=-1) * res)

    def put_chunks(pi, slabs, start, rows, res, m):
        for c, s in enumerate(slabs):
            part = slice(c * rows, (c + 1) * rows)
            put(pi, s, pl.ds(start, rows), res[part], m[part] if exact else None, False)

    def branch16(i, carry):
        bias = bias_ref[0, 0]
        for r in (DIL16_SLABS_PER_TRIP * i + dr for dr in range(DIL16_SLABS_PER_TRIP)):
            for pi in range(DIL_STEP_PAIRS):
                q, k, v = (ref[0, r, :, lanes(pi)].astype(BF16) for ref in (q_ref, k_ref, v_ref))
                res, m = block(q[:BLOCK], k[:BLOCK], v[:BLOCK], bias[:, BLOCK:])
                put(pi, r, slice(0, BLOCK), res, m, True)
                res, m = block(q[BLOCK:], k, v, bias)
                put(pi, r, slice(BLOCK, 2 * BLOCK), res, m, True)
        return carry

    lax.fori_loop(0, N_RES // DIL16_SLABS_PER_TRIP, branch16, 0)

    def strided_block(bi, nslab, r0, n, first_possible):
        rows = BLOCK // nslab
        cur = pl.multiple_of(n * rows, rows)
        prev = pl.multiple_of(jnp.maximum(n - 1, 0) * rows, rows)
        slabs = [r0 + (N_RES // nslab) * c for c in range(nslab)]
        bias = bias_ref[bi, jnp.where(n == 0, 1, 0)] if first_possible else bias_ref[bi, 0]
        for pi in range(DIL_STEP_PAIRS):
            kk = jnp.concatenate([gather(k_ref, slabs, prev, rows, pi),
                                  gather(k_ref, slabs, cur, rows, pi)], axis=0)
            vv = jnp.concatenate([gather(v_ref, slabs, prev, rows, pi),
                                  gather(v_ref, slabs, cur, rows, pi)], axis=0)
            res, m = block(gather(q_ref, slabs, cur, rows, pi), kk, vv, bias)
            put_chunks(pi, slabs, cur, rows, res, m)

    def branch4(i, carry):
        for dn in range(DIL4_BLOCKS_PER_TRIP):
            for r0 in range(4):
                strided_block(1, 4, r0, DIL4_BLOCKS_PER_TRIP * i + dn, dn == 0)
        return carry

    lax.fori_loop(0, RES_ROWS // (BLOCK // 4) // DIL4_BLOCKS_PER_TRIP, branch4, 0)

    def branch1(i, carry):
        for dn in range(DIL1_BLOCKS_PER_TRIP):
            strided_block(2, N_RES, 0, DIL1_BLOCKS_PER_TRIP * i + dn, dn == 0)
        return carry

    lax.fori_loop(0, RES_ROWS // (BLOCK // N_RES) // DIL1_BLOCKS_PER_TRIP, branch1, 0)

    for pi in range(DIL_STEP_PAIRS):
        o_ref[0, :, :, lanes(pi)] = (st[pi, :, :, :LANES] / st[pi, :, :, LANES:]).astype(BF16)


def _dil_attn_call(q, k, v, bias, exact):
    width = DIL_STEP_PAIRS * LANES
    blk = lambda: pl.BlockSpec((1, N_RES, RES_ROWS, width), lambda b, p: (b, 0, 0, p))
    scratch = [pltpu.VMEM((DIL_STEP_PAIRS, N_RES, RES_ROWS, 2 * LANES), F32)]
    if exact:
        scratch.append(pltpu.VMEM((DIL_STEP_PAIRS, N_RES, RES_ROWS, LANES), F32))
    return pl.pallas_call(
        functools.partial(_dil_attn_kernel, exact=exact),
        grid=(BATCH, DIL_HEADS // 2 // DIL_STEP_PAIRS),
        in_specs=[blk(), blk(), blk(), _const_spec(bias.shape)],
        out_specs=blk(),
        out_shape=jax.ShapeDtypeStruct((BATCH, N_RES, RES_ROWS, 512), BF16),
        scratch_shapes=scratch,
        compiler_params=_params(2), name="dil_attn_exact" if exact else "dil_attn",
    )(q, k, v, bias)


def _pool_kernel(u_ref, w_ref, sc_ref, o_ref):
    row0 = lax.broadcasted_iota(jnp.int32, (RES_ROWS, POOL_GROUP), 0) == 0

    def shift(x):
        return jnp.where(row0, 0.0, pltpu.roll(x, 1, axis=0))

    for gi, w in enumerate(POOL_SIZES):
        lanes = slice(gi * POOL_GROUP, (gi + 1) * POOL_GROUP)
        slab = [u_ref[0, r, :, lanes] for r in range(N_RES)]
        pre = [slab[0]]
        for r in range(1, N_RES):
            pre.append(pre[-1] + slab[r])
        tot_sh = shift(pre[-1])
        pooled = []
        for r in range(N_RES):
            if r >= w:
                win = pre[r] - pre[r - w]
            elif r == w - 1:
                win = pre[r]
            else:
                win = pre[r] + (tot_sh - shift(pre[r - w + N_RES]))
            inv = jnp.where(row0, 1.0 / min(r + 1, w), 1.0 / w)
            pooled.append((win * inv - slab[r]).astype(BF16))
        out = _dot(jnp.concatenate(pooled, axis=0), w_ref[gi]) * sc_ref[:, lanes]
        for r in range(N_RES):
            o_ref[0, r, :, lanes] = out[r * RES_ROWS:(r + 1) * RES_ROWS].astype(BF16)


def _pool_call(u, w, sc):
    blk = lambda: pl.BlockSpec((1, N_RES, RES_ROWS, POOL_CH), lambda b: (b, 0, 0, 0))
    return pl.pallas_call(
        _pool_kernel, grid=(BATCH,),
        in_specs=[blk(), _const_spec(w.shape), _const_spec(sc.shape)],
        out_specs=blk(),
        out_shape=jax.ShapeDtypeStruct((BATCH, N_RES, RES_ROWS, POOL_CH), BF16),
        compiler_params=_params(1), name="pool",
    )(u, w, sc)


def _band_bias(nslab, max_dist):
    rows = BLOCK // nslab
    s = np.arange(BLOCK)
    pos = nslab * (s % rows) + s // rows
    dist = pos[:, None] - np.concatenate([pos - BLOCK, pos])[None, :]
    gen = np.where((dist >= 0) & (dist <= max_dist), 0.0, NEG_INF).astype(np.float32)
    fst = gen.copy()
    fst[:, :BLOCK] = NEG_INF
    return np.stack([gen, fst])


def _block_diag_ones():
    i = np.arange(MXU_DIM) // HEAD_DIM
    return jnp.asarray((i[:, None] == i[None, :]).astype(np.float32), dtype=BF16)


def _tile_permutation():
    rows = TM // N_RES
    dst = np.arange(TM)
    src = (dst % rows) * N_RES + dst // rows
    return jnp.asarray((src[:, None] == np.arange(TM)[None, :]).astype(np.float32), dtype=BF16)


def _pad_ff(w, axis):
    pad = [(0, 0), (0, 0)]
    pad[axis] = (0, D_FF_PAD - D_FF)
    return jnp.pad(w, pad).astype(BF16)


def _tile_heads(g, n):
    return jnp.tile(g, n)[None, :]


def _qk_shift(gq, gk):
    return QK_BOUND * jnp.max(jnp.abs(gq)) * jnp.max(jnp.abs(gk))


_EV_HEAD_ORDER = np.array([h for p in range(4) for h in (p, 4 + p)])
_EV_Q_PERM = (_EV_HEAD_ORDER[:, None] * HEAD_DIM + np.arange(HEAD_DIM)[None, :]).reshape(-1)


def kernel(x, norm_g, ffn_w_gate, ffn_w_up, ffn_w_down, ev_w_in, ev_w_out, ev_conv_w, ev_conv_b,
           ev_ln_g, ev_ln_b, ev_q_norm_g, ev_k_norm_g, ev_sinks, od_w_in, od_w_out, od_q_norm_g,
           od_k_norm_g, od_pool_w, od_pool_scale):
    bd = _block_diag_ones()
    ev_bias = jnp.asarray(_band_bias(1, SWA_WINDOW - 1))
    od_bias = jnp.asarray(np.stack([_band_bias(1, BLOCK), _band_bias(4, BLOCK),
                                    _band_bias(16, BLOCK)]))
    to_res16 = _tile_permutation()
    from_res16 = to_res16.T
    x = x.reshape(BATCH * SEQ, D_MODEL)
    for layer in range(DEPTH):
        g = norm_g[layer]
        i = layer // 2
        ffn_w = [(g[2 * j:2 * j + 1], ffn_w_gate[layer, j].astype(BF16),
                  ffn_w_up[layer, j].astype(BF16), ffn_w_down[layer, j].astype(BF16))
                 for j in range(2)]
        x = _ffn_call(x, *ffn_w[0])
        if layer % 2 == 0:
            w_in = jnp.concatenate([ev_w_in[i][:, :1024], ev_w_in[i][:, 1024 + _EV_Q_PERM],
                                    ev_w_in[i][:, 1536:]], axis=1).astype(BF16)
            u, q, k, v = _even_in_call(x, g[1:2], w_in, bd,
                                       _tile_heads(ev_q_norm_g[i], 8), _tile_heads(ev_k_norm_g[i], 2))
            sink = ev_sinks[i][_EV_HEAD_ORDER]
            r3 = lambda t: t.reshape(BATCH, SEQ, t.shape[-1])
            cw = jnp.pad(ev_conv_w[i], ((0, 1), (0, 0)))
            shift = _qk_shift(ev_q_norm_g[i], ev_k_norm_g[i])
            use_shift = jnp.logical_and(shift <= MAX_CONST_SHIFT,
                                        jnp.max(sink) - shift <= MAX_SINK_OVER_SHIFT)
            sink_terms = jnp.repeat(jnp.exp(sink - shift), HEAD_DIM).reshape(SWA_Q_HEADS // 2, 1, LANES)
            mix = lambda sk, bs, exact: _even_mix_call(
                r3(u), r3(q), r3(k), r3(v), cw, ev_conv_b[i][None], ev_ln_g[i][None],
                ev_ln_b[i][None], sk, bs, exact)
            fa, fb = lax.cond(use_shift,
                              lambda: mix(sink_terms, ev_bias - shift, False),
                              lambda: mix(sink.reshape(SWA_Q_HEADS, 1, 1), ev_bias, True))
            w_out = ev_w_out[i].astype(BF16)
            proj = (fa.reshape(-1, 512), fb.reshape(-1, 512), w_out[:512], w_out[512 + _EV_Q_PERM])
            x = _ffn_call(x, *ffn_w[1], proj=proj)
        else:
            q, k, v, u = _odd_in_call(x, g[1:2], to_res16, od_w_in[i].astype(BF16), bd,
                                      _tile_heads(od_q_norm_g[i], 8), _tile_heads(od_k_norm_g[i], 8))
            shift = _qk_shift(od_q_norm_g[i], od_k_norm_g[i])
            att = lax.cond(shift <= MAX_CONST_SHIFT,
                           lambda: _dil_attn_call(q, k, v, od_bias - shift, False),
                           lambda: _dil_attn_call(q, k, v, od_bias, True))
            pool = _pool_call(u, od_pool_w[i].astype(BF16), od_pool_scale[i][None])
            w_out = od_w_out[i].astype(BF16)
            x = _ffn_call(x, *ffn_w[1], proj=(att, pool, w_out[:512], w_out[512:]), perm=from_res16)
    return x.reshape(BATCH, SEQ, D_MODEL)
```

```python
import functools

import jax
import jax.numpy as jnp
import numpy as np
from jax import lax
from jax.experimental import pallas as pl
from jax.experimental.pallas import tpu as pltpu

F32 = jnp.float32
BF16 = jnp.bfloat16

D_MODEL = 1024
BATCH = 8
SEQ = 4096
DEPTH = 4
HEAD_DIM = 64
D_FF = 2752
EPS = 1e-6
NEG_INF = -1e30
BLOCK = 128
CONV_CH = 512
CONV_WIDTH = 31
SWA_Q_HEADS = 8
SWA_WINDOW = 128
EVEN_IN = 1792
DIL_HEADS = 8
POOL_CH = 512
POOL_SIZES = (2, 4, 8, 16)
POOL_GROUP = 128
ODD_IN = 2048
QK_SCALE = HEAD_DIM ** -0.5

LANES = 128
SUBLANES = 8
MXU_DIM = 256
N_RES = 16
RES_ROWS = SEQ // N_RES
TM = 512
EV_CHUNK = 1024
CONV_HALO = 32
DIL_STEP_PAIRS = 2
DIL16_SLABS_PER_TRIP = 4
DIL4_BLOCKS_PER_TRIP = 2
DIL1_BLOCKS_PER_TRIP = 8
QK_BOUND = HEAD_DIM * QK_SCALE * 1.03
MAX_CONST_SHIFT = 20.0
MAX_SINK_OVER_SHIFT = 60.0
VMEM_LIMIT = 56 * 1024 * 1024


def _params(n_axes):
    return pltpu.CompilerParams(dimension_semantics=("arbitrary",) * n_axes,
                                vmem_limit_bytes=VMEM_LIMIT)


def _const_spec(shape):
    nd = len(shape)
    return pl.BlockSpec(shape, lambda *_: (0,) * nd, pipeline_mode=pl.Buffered(1))


def _rms(x, g):
    return x * lax.rsqrt(jnp.mean(x * x, axis=-1, keepdims=True) + EPS) * g


def _sigmoid(x, one=1.0):
    return 1.0 / (one + jnp.exp(-x))


def _dot(a, b):
    return jnp.dot(a, b, preferred_element_type=F32)


def _dot_nt(a, b):
    return lax.dot_general(a, b, (((1,), (1,)), ((), ())), preferred_element_type=F32)


def _head_sumsq(x, bd_ref):
    sq = (x * x).astype(BF16)
    bd = bd_ref[...]
    outs = [_dot(sq[:, c * MXU_DIM:(c + 1) * MXU_DIM], bd) for c in range(x.shape[1] // MXU_DIM)]
    return outs[0] if len(outs) == 1 else jnp.concatenate(outs, axis=-1)


def _head_rms(x, ss, g):
    return x * lax.rsqrt(ss * (1.0 / HEAD_DIM) + EPS) * g


def _permute_rows(perm_ref, val):
    return _dot(perm_ref[...], val).astype(BF16)


def _conv_fill(ucat, halo, u_ref):
    ucat[:CONV_HALO] = halo
    ucat[CONV_HALO:CONV_HALO + TM] = u_ref[...]
    ucat[CONV_HALO + TM:] = jnp.zeros((SUBLANES, CONV_CH), F32)


def _conv_block(blk, ucat, cw_ref, cb_ref, lng_ref, lnb_ref, out_ref):
    off = CONV_HALO - (CONV_WIDTH - 1)
    base = blk * BLOCK
    wins = [ucat[base + SUBLANES * eh:base + SUBLANES * eh + BLOCK + SUBLANES, :]
            for eh in range(CONV_HALO // SUBLANES + 1)]
    acc = jnp.broadcast_to(cb_ref[...], (BLOCK, CONV_CH))
    for el in range(SUBLANES):
        part = None
        for eh, win in enumerate(wins):
            j = SUBLANES * eh + el - off
            if 0 <= j < CONV_WIDTH:
                term = cw_ref[j:j + 1, :] * win
                part = term if part is None else part + term
        acc = acc + part[el:el + BLOCK]
    mu = jnp.mean(acc, axis=-1, keepdims=True)
    d = acc - mu
    var = jnp.mean(d * d, axis=-1, keepdims=True)
    y = d * lax.rsqrt(var + EPS) * lng_ref[...] + lnb_ref[...]
    out = y * _sigmoid(y)
    out_ref[base:base + BLOCK, :] = out.astype(BF16)
    return _dependent_zero(out[:SUBLANES, :LANES])


def _dependent_zero(t):
    bits = lax.bitcast_convert_type(t, jnp.int32)
    zero = lax.shift_right_logical(lax.shift_right_logical(bits, 16), 16)
    return jnp.max(zero.astype(F32), axis=(0, 1), keepdims=True)


def _ffn_kernel(*refs, mode):
    if mode == "even":
        (x_ref, u_ref, un_ref, cw_ref, cb_ref, lng_ref, lnb_ref, fb_ref, woa_ref, wob_ref,
         g_ref, wg_ref, wu_ref, wd_ref, o_ref, ucat, fa_scr) = refs
    elif mode == "odd":
        x_ref, fa_ref, fb_ref, perm_ref, woa_ref, wob_ref, g_ref, wg_ref, wu_ref, wd_ref, o_ref = refs
    else:
        x_ref, g_ref, wg_ref, wu_ref, wd_ref, o_ref = refs
    x = x_ref[...]
    conv_next = lambda blk: 0.0
    if mode == "even":
        i = pl.program_id(0)
        conv = functools.partial(_conv_block, ucat=ucat, cw_ref=cw_ref, cb_ref=cb_ref,
                                 lng_ref=lng_ref, lnb_ref=lnb_ref)

        @pl.when(i == 0)
        def _():
            _conv_fill(ucat, jnp.zeros((CONV_HALO, CONV_CH), F32), u_ref)
            for blk in range(TM // BLOCK):
                conv(blk, out_ref=fa_scr.at[0])

        seq_start = lax.rem(i + 1, SEQ // TM) == 0
        _conv_fill(ucat, jnp.where(seq_start, 0.0, u_ref[TM - CONV_HALO:, :]), un_ref)
        conv_next = functools.partial(conv, out_ref=fa_scr.at[lax.rem(i + 1, 2)])
        x = x + _dot(fa_scr[lax.rem(i, 2)], woa_ref[...]) + _dot(fb_ref[...], wob_ref[...])
    elif mode == "odd":
        feats = [_permute_rows(perm_ref, f_ref[0].reshape(TM, f_ref.shape[-1]))
                 for f_ref in (fa_ref, fb_ref)]
        x = x + _dot(feats[0], woa_ref[...]) + _dot(feats[1], wob_ref[...])
    xn = _rms(x, g_ref[...]).astype(BF16)
    h = _dot(xn, wg_ref[...])
    u = _dot(xn, wu_ref[...])
    one = 1.0 + (conv_next(0) + conv_next(1))
    a = (h * _sigmoid(h, one) * u).astype(BF16)
    y = _dot(a, wd_ref[...])
    half = 0.5 + (conv_next(2) + conv_next(3))
    o_ref[...] = x + half * y


def _row_spec(width):
    return pl.BlockSpec((TM, width), lambda i: (i, 0))


def _res16_spec(width):
    tiles = SEQ // TM
    return pl.BlockSpec((1, N_RES, TM // N_RES, width), lambda i: (i // tiles, 0, i % tiles, 0))


def _stack_spec(w, layer, j):
    return pl.BlockSpec((None, None) + w.shape[2:], lambda i: (layer, j, 0, 0),
                        pipeline_mode=pl.Buffered(1))


def _ffn_call(x, g, ffn_w, layer, j, mode="plain", extra=()):
    n_tiles = BATCH * SEQ // TM
    in_specs = [_row_spec(D_MODEL)]
    args = [x]
    scratch = []
    if mode == "odd":
        fa, fb, perm, woa, wob = extra
        in_specs += [_res16_spec(fa.shape[-1]), _res16_spec(fb.shape[-1])]
        in_specs += [_const_spec(t.shape) for t in (perm, woa, wob)]
        args += [fa, fb, perm, woa, wob]
    elif mode == "even":
        u, cw, cb, lng, lnb, fb, woa, wob = extra
        nxt = pl.BlockSpec((TM, CONV_CH), lambda i: (jnp.minimum(i + 1, n_tiles - 1), 0))
        in_specs += [_row_spec(CONV_CH), nxt]
        in_specs += [_const_spec(t.shape) for t in (cw, cb, lng, lnb)]
        in_specs += [_row_spec(fb.shape[-1]), _const_spec(woa.shape), _const_spec(wob.shape)]
        args += [u, u, cw, cb, lng, lnb, fb, woa, wob]
        scratch = [pltpu.VMEM((CONV_HALO + TM + SUBLANES, CONV_CH), F32),
                   pltpu.VMEM((2, TM, CONV_CH), BF16)]
    in_specs += [_const_spec(g.shape)] + [_stack_spec(w, layer, j) for w in ffn_w]
    args += [g, *ffn_w]
    return pl.pallas_call(
        functools.partial(_ffn_kernel, mode=mode),
        grid=(n_tiles,), in_specs=in_specs, out_specs=_row_spec(D_MODEL),
        out_shape=jax.ShapeDtypeStruct(x.shape, F32), scratch_shapes=scratch,
        compiler_params=_params(1), name="ffn_" + mode,
    )(*args)


def _even_in_kernel(x_ref, g_ref, w_ref, bd_ref, gq_ref, gk_ref, u_ref, q_ref, k_ref, v_ref):
    hn = _rms(x_ref[...], g_ref[...]).astype(BF16)
    proj = _dot(hn, w_ref[...])
    a_val = proj[:, :CONV_CH]
    a_gate = proj[:, CONV_CH:2 * CONV_CH]
    u_ref[...] = a_val * _sigmoid(a_gate)
    qkv = proj[:, 2 * CONV_CH:]
    ss = _head_sumsq(qkv, bd_ref)
    q = qkv[:, :512]
    k = qkv[:, 512:640]
    q_ref[...] = (_head_rms(q, ss[:, :512], gq_ref[...]) * QK_SCALE).astype(BF16)
    k_ref[...] = _head_rms(k, ss[:, 512:640], gk_ref[...]).astype(BF16)
    v_ref[...] = qkv[:, 640:].astype(BF16)


def _even_in_call(x, g, w_in, bd, gq, gk):
    n = x.shape[0]
    row = lambda width: pl.BlockSpec((TM, width), lambda i: (i, 0))
    return pl.pallas_call(
        _even_in_kernel, grid=(n // TM,),
        in_specs=[row(D_MODEL), _const_spec(g.shape), _const_spec(w_in.shape),
                  _const_spec(bd.shape), _const_spec(gq.shape), _const_spec(gk.shape)],
        out_specs=[row(512), row(512), row(128), row(128)],
        out_shape=[jax.ShapeDtypeStruct((n, 512), F32), jax.ShapeDtypeStruct((n, 512), BF16),
                   jax.ShapeDtypeStruct((n, 128), BF16), jax.ShapeDtypeStruct((n, 128), BF16)],
        compiler_params=_params(1), name="even_in",
    )(x, g, w_in, bd, gq, gk)


def _pv_operand(vv):
    lo = lax.broadcasted_iota(jnp.int32, vv.shape, 1) < HEAD_DIM
    zero = jnp.zeros_like(vv)
    ones_lo = jnp.where(lo, 1.0, 0.0)
    top = jnp.concatenate([jnp.where(lo, vv, zero), ones_lo.astype(BF16)], axis=-1)
    bot = jnp.concatenate([jnp.where(lo, zero, vv), (1.0 - ones_lo).astype(BF16)], axis=-1)
    return jnp.concatenate([top, bot], axis=0)


def _attend(s, bias, vv, exact, sink=None):
    nh = s.shape[0] // BLOCK
    s3 = s.reshape(nh, BLOCK, s.shape[1]) + bias[None]
    lo_q = lax.broadcasted_iota(jnp.int32, (BLOCK, LANES), 1) < HEAD_DIM
    m = None
    if exact:
        m = jnp.max(s3, axis=-1, keepdims=True)
        if sink is not None:
            m = jnp.maximum(m, sink)
        s3 = s3 - m
        if sink is not None:
            sink = jnp.exp(sink - m)
    p = jnp.exp(s3)
    vext = _pv_operand(vv)
    outs = []
    for pr in range(nh // 2):
        pcat = jnp.concatenate([p[2 * pr], p[2 * pr + 1]], axis=-1).astype(BF16)
        res = _dot(pcat, vext)
        if sink is not None:
            term = jnp.where(lo_q, sink[2 * pr], sink[2 * pr + 1]) if exact else sink[pr]
            res = jnp.concatenate([res[:, :LANES], res[:, LANES:] + term], axis=-1)
        m_pair = jnp.where(lo_q, m[2 * pr], m[2 * pr + 1]) if exact else None
        outs.append((res, m_pair))
    return outs


def _stack_pairs(q_pairs):
    lo = lax.broadcasted_iota(jnp.int32, (BLOCK, LANES), 1) < HEAD_DIM
    parts = []
    for q in q_pairs:
        zero = jnp.zeros_like(q)
        parts += [jnp.where(lo, q, zero), jnp.where(lo, zero, q)]
    return jnp.concatenate(parts, axis=0)


def _even_mix_kernel(q_ref, k_ref, kh_ref, v_ref, vh_ref, sink_ref, bias_ref, fb_ref,
                     kcat, vcat, *, exact):
    first = pl.program_id(1) == 0
    kcat[:BLOCK] = kh_ref[0]
    kcat[BLOCK:] = k_ref[0]
    vcat[:BLOCK] = vh_ref[0]
    vcat[BLOCK:] = v_ref[0]
    sink = sink_ref[...]

    def attn_block(i, carry):
        base = pl.multiple_of(i * BLOCK, BLOCK)
        kk = kcat[pl.ds(base, 2 * BLOCK), :]
        vv = vcat[pl.ds(base, 2 * BLOCK), :]
        qs = _stack_pairs([q_ref[0, pl.ds(base, BLOCK), pr * LANES:(pr + 1) * LANES]
                           for pr in range(SWA_Q_HEADS // 2)])
        bias = bias_ref[jnp.where(jnp.logical_and(first, i == 0), 1, 0)]
        s = _dot_nt(qs, kk)
        for pr, (res, _) in enumerate(_attend(s, bias, vv, exact, sink)):
            out = res[:, :LANES] / res[:, LANES:]
            fb_ref[0, pl.ds(base, BLOCK), pr * LANES:(pr + 1) * LANES] = out.astype(BF16)
        return carry

    lax.fori_loop(0, EV_CHUNK // BLOCK, attn_block, 0, unroll=2)


def _even_mix_call(q, k, v, sink, bias, exact):
    nchunk = SEQ // EV_CHUNK
    cur = lambda width: pl.BlockSpec((1, EV_CHUNK, width), lambda b, c: (b, c, 0))
    halo = pl.BlockSpec((1, BLOCK, LANES),
                        lambda b, c: (b, jnp.maximum(c * (EV_CHUNK // BLOCK) - 1, 0), 0))
    return pl.pallas_call(
        functools.partial(_even_mix_kernel, exact=exact), grid=(BATCH, nchunk),
        in_specs=[cur(512), cur(128), halo, cur(128), halo, _const_spec(sink.shape),
                  _const_spec(bias.shape)],
        out_specs=cur(512),
        out_shape=jax.ShapeDtypeStruct((BATCH, SEQ, 512), BF16),
        scratch_shapes=[pltpu.VMEM((BLOCK + EV_CHUNK, LANES), BF16),
                        pltpu.VMEM((BLOCK + EV_CHUNK, LANES), BF16)],
        compiler_params=_params(2), name="even_mix_exact" if exact else "even_mix",
    )(q, k, k, v, v, sink, bias)


def _odd_in_kernel(x_ref, g_ref, perm_ref, w_ref, bd_ref, gq_ref, gk_ref,
                   q_ref, k_ref, v_ref, u_ref):
    hn = _rms(x_ref[...], g_ref[...]).astype(BF16)
    hn = _permute_rows(perm_ref, hn)
    proj = _dot(hn, w_ref[...])
    qk = proj[:, :1024]
    ss = _head_sumsq(qk, bd_ref)
    q = _head_rms(qk[:, :512], ss[:, :512], gq_ref[...]) * QK_SCALE
    k = _head_rms(qk[:, 512:], ss[:, 512:], gk_ref[...])
    v = proj[:, 1024:1536]
    u = proj[:, 1536:]
    rows = TM // N_RES
    for ref, val in ((q_ref, q), (k_ref, k), (v_ref, v), (u_ref, u)):
        for r in range(N_RES):
            ref[0, r] = val[r * rows:(r + 1) * rows]


def _odd_in_call(x, g, perm, w_in, bd, gq, gk):
    shp = jax.ShapeDtypeStruct((BATCH, N_RES, RES_ROWS, 512), F32)
    return pl.pallas_call(
        _odd_in_kernel, grid=(BATCH * SEQ // TM,),
        in_specs=[_row_spec(D_MODEL), _const_spec(g.shape), _const_spec(perm.shape),
                  _const_spec(w_in.shape), _const_spec(bd.shape), _const_spec(gq.shape),
                  _const_spec(gk.shape)],
        out_specs=[_res16_spec(512)] * 4,
        out_shape=[shp] * 4,
        compiler_params=_params(1), name="odd_in",
    )(x, g, perm, w_in, bd, gq, gk)


def _dil_attn_kernel(q_ref, k_ref, v_ref, bias_ref, o_ref, st, *m_scratch, exact):
    m_s = m_scratch[0] if exact else None

    def lanes(pi):
        return slice(pi * LANES, (pi + 1) * LANES)

    def gather(ref, slabs, start, rows, pi):
        parts = [ref[0, s, pl.ds(start, rows), lanes(pi)] for s in slabs]
        return jnp.concatenate(parts, axis=0).astype(BF16)

    def block(q, kk, vv, bias):
        return _attend(_dot_nt(_stack_pairs([q]), kk), bias, vv, exact)[0]

    def put(pi, slab, rows, res, m, init):
        idx = (pi, slab, rows, slice(None))
        if init:
            st[idx] = res
            if exact:
                m_s[idx] = m
        elif not exact:
            st[idx] += res
        else:
            m_old = m_s[idx]
            m_new = jnp.maximum(m_old, m)
            a = jnp.exp(m_old - m_new)
            b = jnp.exp(m - m_new)
            m_s[idx] = m_new
            st[idx] = (jnp.concatenate([a, a], axis=-1) * st[idx]
                       + jnp.concatenate([b, b], axis=-1) * res)

    def put_chunks(pi, slabs, start, rows, res, m):
        for c, s in enumerate(slabs):
            part = slice(c * rows, (c + 1) * rows)
            put(pi, s, pl.ds(start, rows), res[part], m[part] if exact else None, False)

    def branch16(i, carry):
        bias = bias_ref[0, 0]
        for r in (DIL16_SLABS_PER_TRIP * i + dr for dr in range(DIL16_SLABS_PER_TRIP)):
            for pi in range(DIL_STEP_PAIRS):
                q, k, v = (ref[0, r, :, lanes(pi)].astype(BF16) for ref in (q_ref, k_ref, v_ref))
                res, m = block(q[:BLOCK], k[:BLOCK], v[:BLOCK], bias[:, BLOCK:])
                put(pi, r, slice(0, BLOCK), res, m, True)
                res, m = block(q[BLOCK:], k, v, bias)
                put(pi, r, slice(BLOCK, 2 * BLOCK), res, m, True)
        return carry

    lax.fori_loop(0, N_RES // DIL16_SLABS_PER_TRIP, branch16, 0)

    def strided_block(bi, nslab, r0, n, first_possible):
        rows = BLOCK // nslab
        cur = pl.multiple_of(n * rows, rows)
        prev = pl.multiple_of(jnp.maximum(n - 1, 0) * rows, rows)
        slabs = [r0 + (N_RES // nslab) * c for c in range(nslab)]
        bias = bias_ref[bi, jnp.where(n == 0, 1, 0)] if first_possible else bias_ref[bi, 0]
        for pi in range(DIL_STEP_PAIRS):
            kk = jnp.concatenate([gather(k_ref, slabs, prev, rows, pi),
                                  gather(k_ref, slabs, cur, rows, pi)], axis=0)
            vv = jnp.concatenate([gather(v_ref, slabs, prev, rows, pi),
                                  gather(v_ref, slabs, cur, rows, pi)], axis=0)
            res, m = block(gather(q_ref, slabs, cur, rows, pi), kk, vv, bias)
            put_chunks(pi, slabs, cur, rows, res, m)

    def branch4(i, carry):
        for dn in range(DIL4_BLOCKS_PER_TRIP):
            for r0 in range(4):
                strided_block(1, 4, r0, DIL4_BLOCKS_PER_TRIP * i + dn, dn == 0)
        return carry

    lax.fori_loop(0, RES_ROWS // (BLOCK // 4) // DIL4_BLOCKS_PER_TRIP, branch4, 0)

    def branch1(i, carry):
        for dn in range(DIL1_BLOCKS_PER_TRIP):
            strided_block(2, N_RES, 0, DIL1_BLOCKS_PER_TRIP * i + dn, dn == 0)
        return carry

    lax.fori_loop(0, RES_ROWS // (BLOCK // N_RES) // DIL1_BLOCKS_PER_TRIP, branch1, 0)

    for pi in range(DIL_STEP_PAIRS):
        o_ref[0, :, :, lanes(pi)] = (st[pi, :, :, :LANES] / st[pi, :, :, LANES:]).astype(BF16)


def _dil_attn_call(q, k, v, bias, exact):
    width = DIL_STEP_PAIRS * LANES
    blk = lambda: pl.BlockSpec((1, N_RES, RES_ROWS, width), lambda b, p: (b, 0, 0, p))
    scratch = [pltpu.VMEM((DIL_STEP_PAIRS, N_RES, RES_ROWS, 2 * LANES), F32)]
    if exact:
        scratch.append(pltpu.VMEM((DIL_STEP_PAIRS, N_RES, RES_ROWS, LANES), F32))
    return pl.pallas_call(
        functools.partial(_dil_attn_kernel, exact=exact),
        grid=(BATCH, DIL_HEADS // 2 // DIL_STEP_PAIRS),
        in_specs=[blk(), blk(), blk(), _const_spec(bias.shape)],
        out_specs=blk(),
        out_shape=jax.ShapeDtypeStruct((BATCH, N_RES, RES_ROWS, 512), BF16),
        scratch_shapes=scratch,
        compiler_params=_params(2), name="dil_attn_exact" if exact else "dil_attn",
    )(q, k, v, bias)


def _pool_kernel(u_ref, w_ref, sc_ref, o_ref):
    row0 = lax.broadcasted_iota(jnp.int32, (RES_ROWS, POOL_GROUP), 0) == 0

    def shift(x):
        return jnp.where(row0, 0.0, pltpu.roll(x, 1, axis=0))

    for gi, w in enumerate(POOL_SIZES):
        lanes = slice(gi * POOL_GROUP, (gi + 1) * POOL_GROUP)
        slab = [u_ref[0, r, :, lanes] for r in range(N_RES)]
        pre = [slab[0]]
        for r in range(1, N_RES):
            pre.append(pre[-1] + slab[r])
        tot_sh = shift(pre[-1])
        pooled = []
        for r in range(N_RES):
            if r >= w:
                win = pre[r] - pre[r - w]
            elif r == w - 1:
                win = pre[r]
            else:
                win = pre[r] + (tot_sh - shift(pre[r - w + N_RES]))
            inv = jnp.where(row0, 1.0 / min(r + 1, w), 1.0 / w)
            pooled.append((win * inv - slab[r]).astype(BF16))
        out = _dot(jnp.concatenate(pooled, axis=0), w_ref[gi]) * sc_ref[:, lanes]
        for r in range(N_RES):
            o_ref[0, r, :, lanes] = out[r * RES_ROWS:(r + 1) * RES_ROWS].astype(BF16)


def _pool_call(u, w, sc):
    blk = lambda: pl.BlockSpec((1, N_RES, RES_ROWS, POOL_CH), lambda b: (b, 0, 0, 0))
    return pl.pallas_call(
        _pool_kernel, grid=(BATCH,),
        in_specs=[blk(), _const_spec(w.shape), _const_spec(sc.shape)],
        out_specs=blk(),
        out_shape=jax.ShapeDtypeStruct((BATCH, N_RES, RES_ROWS, POOL_CH), BF16),
        compiler_params=_params(1), name="pool",
    )(u, w, sc)


def _band_bias(nslab, max_dist):
    rows = BLOCK // nslab
    s = np.arange(BLOCK)
    pos = nslab * (s % rows) + s // rows
    dist = pos[:, None] - np.concatenate([pos - BLOCK, pos])[None, :]
    gen = np.where((dist >= 0) & (dist <= max_dist), 0.0, NEG_INF).astype(np.float32)
    fst = gen.copy()
    fst[:, :BLOCK] = NEG_INF
    return np.stack([gen, fst])


def _block_diag_ones():
    i = np.arange(MXU_DIM) // HEAD_DIM
    return jnp.asarray((i[:, None] == i[None, :]).astype(np.float32), dtype=BF16)


def _tile_permutation():
    rows = TM // N_RES
    dst = np.arange(TM)
    src = (dst % rows) * N_RES + dst // rows
    return jnp.asarray((src[:, None] == np.arange(TM)[None, :]).astype(np.float32), dtype=BF16)


def _tile_heads(g, n):
    return jnp.tile(g, n)[None, :]


def _qk_shift(gq, gk):
    return QK_BOUND * jnp.max(jnp.abs(gq)) * jnp.max(jnp.abs(gk))


_EV_HEAD_ORDER = np.array([h for p in range(4) for h in (p, 4 + p)])
_EV_Q_PERM = (_EV_HEAD_ORDER[:, None] * HEAD_DIM + np.arange(HEAD_DIM)[None, :]).reshape(-1)


def kernel(x, norm_g, ffn_w_gate, ffn_w_up, ffn_w_down, ev_w_in, ev_w_out, ev_conv_w, ev_conv_b,
           ev_ln_g, ev_ln_b, ev_q_norm_g, ev_k_norm_g, ev_sinks, od_w_in, od_w_out, od_q_norm_g,
           od_k_norm_g, od_pool_w, od_pool_scale):
    bd = _block_diag_ones()
    ev_bias = jnp.asarray(_band_bias(1, SWA_WINDOW - 1))
    od_bias = jnp.asarray(np.stack([_band_bias(1, BLOCK), _band_bias(4, BLOCK),
                                    _band_bias(16, BLOCK)]))
    to_res16 = _tile_permutation()
    from_res16 = to_res16.T
    ffn_w = (ffn_w_gate.astype(BF16), ffn_w_up.astype(BF16), ffn_w_down.astype(BF16))
    x = x.reshape(BATCH * SEQ, D_MODEL)
    for layer in range(DEPTH):
        g = norm_g[layer]
        i = layer // 2
        x = _ffn_call(x, g[0:1], ffn_w, layer, 0)
        if layer % 2 == 0:
            w_in = jnp.concatenate([ev_w_in[i][:, :1024], ev_w_in[i][:, 1024 + _EV_Q_PERM],
                                    ev_w_in[i][:, 1536:]], axis=1).astype(BF16)
            u, q, k, v = _even_in_call(x, g[1:2], w_in, bd,
                                       _tile_heads(ev_q_norm_g[i], 8), _tile_heads(ev_k_norm_g[i], 2))
            sink = ev_sinks[i][_EV_HEAD_ORDER]
            r3 = lambda t: t.reshape(BATCH, SEQ, t.shape[-1])
            shift = _qk_shift(ev_q_norm_g[i], ev_k_norm_g[i])
            use_shift = jnp.logical_and(shift <= MAX_CONST_SHIFT,
                                        jnp.max(sink) - shift <= MAX_SINK_OVER_SHIFT)
            sink_terms = jnp.repeat(jnp.exp(sink - shift), HEAD_DIM).reshape(SWA_Q_HEADS // 2, 1, LANES)
            mix = lambda sk, bs, exact: _even_mix_call(r3(q), r3(k), r3(v), sk, bs, exact)
            fb = lax.cond(use_shift,
                          lambda: mix(sink_terms, ev_bias - shift, False),
                          lambda: mix(sink.reshape(SWA_Q_HEADS, 1, 1), ev_bias, True))
            w_out = ev_w_out[i].astype(BF16)
            extra = (u, jnp.pad(ev_conv_w[i], ((0, 1), (0, 0))), ev_conv_b[i][None],
                     ev_ln_g[i][None], ev_ln_b[i][None], fb.reshape(-1, 512),
                     w_out[:512], w_out[512 + _EV_Q_PERM])
            x = _ffn_call(x, g[2:3], ffn_w, layer, 1, "even", extra)
        else:
            q, k, v, u = _odd_in_call(x, g[1:2], to_res16, od_w_in[i].astype(BF16), bd,
                                      _tile_heads(od_q_norm_g[i], 8), _tile_heads(od_k_norm_g[i], 8))
            shift = _qk_shift(od_q_norm_g[i], od_k_norm_g[i])
            att = lax.cond(shift <= MAX_CONST_SHIFT,
                           lambda: _dil_attn_call(q, k, v, od_bias - shift, False),
                           lambda: _dil_attn_call(q, k, v, od_bias, True))
            pool = _pool_call(u, od_pool_w[i].astype(BF16), od_pool_scale[i][None])
            w_out = od_w_out[i].astype(BF16)
            extra = (att, pool, from_res16, w_out[:512], w_out[512:])
            x = _ffn_call(x, g[2:3], ffn_w, layer, 1, "odd", extra)
    return x.reshape(BATCH, SEQ, D_MODEL)
```

```python
import functools

import jax
import jax.numpy as jnp
import numpy as np
from jax import lax
from jax.experimental import pallas as pl
from jax.experimental.pallas import tpu as pltpu

F32 = jnp.float32
BF16 = jnp.bfloat16

D_MODEL = 1024
BATCH = 8
SEQ = 4096
DEPTH = 4
HEAD_DIM = 64
D_FF = 2752
EPS = 1e-6
NEG_INF = -1e30
BLOCK = 128
CONV_CH = 512
CONV_WIDTH = 31
SWA_Q_HEADS = 8
SWA_WINDOW = 128
EVEN_IN = 1792
DIL_HEADS = 8
POOL_CH = 512
POOL_SIZES = (2, 4, 8, 16)
POOL_GROUP = 128
ODD_IN = 2048
QK_SCALE = HEAD_DIM ** -0.5

LANES = 128
SUBLANES = 8
MXU_DIM = 256
N_RES = 16
RES_ROWS = SEQ // N_RES
TM = 512
EV_CHUNK = 1024
CONV_HALO = 32
DIL_STEP_PAIRS = 2
DIL16_SLABS_PER_TRIP = 8
DIL4_BLOCKS_PER_TRIP = 4
DIL1_BLOCKS_PER_TRIP = 16
QK_BOUND = HEAD_DIM * QK_SCALE * 1.03
MAX_CONST_SHIFT = 20.0
MAX_SINK_OVER_SHIFT = 60.0
VMEM_LIMIT = 56 * 1024 * 1024


def _params(n_axes):
    return pltpu.CompilerParams(dimension_semantics=("arbitrary",) * n_axes,
                                vmem_limit_bytes=VMEM_LIMIT)


def _const_spec(shape):
    nd = len(shape)
    return pl.BlockSpec(shape, lambda *_: (0,) * nd, pipeline_mode=pl.Buffered(1))


def _rms(x, g):
    return x * lax.rsqrt(jnp.mean(x * x, axis=-1, keepdims=True) + EPS) * g


def _sigmoid(x, one=1.0):
    return 1.0 / (one + jnp.exp(-x))


def _dot(a, b):
    return jnp.dot(a, b, preferred_element_type=F32)


def _dot_nt(a, b):
    return lax.dot_general(a, b, (((1,), (1,)), ((), ())), preferred_element_type=F32)


def _head_sumsq(x, bd_ref):
    sq = (x * x).astype(BF16)
    bd = bd_ref[...]
    outs = [_dot(sq[:, c * MXU_DIM:(c + 1) * MXU_DIM], bd) for c in range(x.shape[1] // MXU_DIM)]
    return outs[0] if len(outs) == 1 else jnp.concatenate(outs, axis=-1)


def _head_rms(x, ss, g):
    return x * lax.rsqrt(ss * (1.0 / HEAD_DIM) + EPS) * g


def _permute_rows(perm_ref, val):
    return _dot(perm_ref[...], val).astype(BF16)


def _conv_fill(ucat, halo, u_ref):
    ucat[:CONV_HALO] = halo
    ucat[CONV_HALO:CONV_HALO + TM] = u_ref[...]
    ucat[CONV_HALO + TM:] = jnp.zeros((SUBLANES, CONV_CH), F32)


def _conv_block(blk, ucat, cw_ref, cb_ref, lng_ref, lnb_ref, out_ref):
    off = CONV_HALO - (CONV_WIDTH - 1)
    base = blk * BLOCK
    wins = [ucat[base + SUBLANES * eh:base + SUBLANES * eh + BLOCK + SUBLANES, :]
            for eh in range(CONV_HALO // SUBLANES + 1)]
    acc = jnp.broadcast_to(cb_ref[...], (BLOCK, CONV_CH))
    for el in range(SUBLANES):
        part = None
        for eh, win in enumerate(wins):
            j = SUBLANES * eh + el - off
            if 0 <= j < CONV_WIDTH:
                term = cw_ref[j:j + 1, :] * win
                part = term if part is None else part + term
        acc = acc + part[el:el + BLOCK]
    mu = jnp.mean(acc, axis=-1, keepdims=True)
    d = acc - mu
    var = jnp.mean(d * d, axis=-1, keepdims=True)
    y = d * lax.rsqrt(var + EPS) * lng_ref[...] + lnb_ref[...]
    out = y * _sigmoid(y)
    out_ref[base:base + BLOCK, :] = out.astype(BF16)
    return _dependent_zero(out[:SUBLANES, :LANES])


def _dependent_zero(t):
    bits = lax.bitcast_convert_type(t, jnp.int32)
    zero = lax.shift_right_logical(lax.shift_right_logical(bits, 16), 16)
    return jnp.max(zero.astype(F32), axis=(0, 1), keepdims=True)


def _ffn_kernel(*refs, mode):
    if mode == "even":
        (x_ref, u_ref, un_ref, cw_ref, cb_ref, lng_ref, lnb_ref, fb_ref, woa_ref, wob_ref,
         g_ref, wg_ref, wu_ref, wd_ref, o_ref, ucat, fa_scr) = refs
    elif mode == "odd":
        x_ref, fa_ref, fb_ref, perm_ref, woa_ref, wob_ref, g_ref, wg_ref, wu_ref, wd_ref, o_ref = refs
    else:
        x_ref, g_ref, wg_ref, wu_ref, wd_ref, o_ref = refs
    if mode == "plain":
        for part in range(2):
            rows = slice(part * TM // 2, (part + 1) * TM // 2)
            xp = x_ref[rows, :]
            xn = _rms(xp, g_ref[...]).astype(BF16)
            h = _dot(xn, wg_ref[...])
            u = _dot(xn, wu_ref[...])
            a = (h * _sigmoid(h) * u).astype(BF16)
            o_ref[rows, :] = xp + 0.5 * _dot(a, wd_ref[...])
        return
    x = x_ref[...]
    conv_next = lambda blk: 0.0
    if mode == "even":
        i = pl.program_id(0)
        conv = functools.partial(_conv_block, ucat=ucat, cw_ref=cw_ref, cb_ref=cb_ref,
                                 lng_ref=lng_ref, lnb_ref=lnb_ref)

        @pl.when(i == 0)
        def _():
            _conv_fill(ucat, jnp.zeros((CONV_HALO, CONV_CH), F32), u_ref)
            for blk in range(TM // BLOCK):
                conv(blk, out_ref=fa_scr.at[0])

        seq_start = lax.rem(i + 1, SEQ // TM) == 0
        _conv_fill(ucat, jnp.where(seq_start, 0.0, u_ref[TM - CONV_HALO:, :]), un_ref)
        conv_next = functools.partial(conv, out_ref=fa_scr.at[lax.rem(i + 1, 2)])
        x = x + _dot(fa_scr[lax.rem(i, 2)], woa_ref[...]) + _dot(fb_ref[...], wob_ref[...])
    elif mode == "odd":
        feats = [_permute_rows(perm_ref, f_ref[0].reshape(TM, f_ref.shape[-1]))
                 for f_ref in (fa_ref, fb_ref)]
        x = x + _dot(feats[0], woa_ref[...]) + _dot(feats[1], wob_ref[...])
    xn = _rms(x, g_ref[...]).astype(BF16)
    h = _dot(xn, wg_ref[...])
    u = _dot(xn, wu_ref[...])
    one = 1.0 + (conv_next(0) + conv_next(1))
    a = (h * _sigmoid(h, one) * u).astype(BF16)
    y = _dot(a, wd_ref[...])
    half = 0.5 + (conv_next(2) + conv_next(3))
    o_ref[...] = x + half * y


def _row_spec(width):
    return pl.BlockSpec((TM, width), lambda i: (i, 0))


def _res16_spec(width):
    tiles = SEQ // TM
    return pl.BlockSpec((1, N_RES, TM // N_RES, width), lambda i: (i // tiles, 0, i % tiles, 0))


def _stack_spec(w, layer, j):
    return pl.BlockSpec((None, None) + w.shape[2:], lambda i: (layer, j, 0, 0),
                        pipeline_mode=pl.Buffered(1))


def _ffn_call(x, g, ffn_w, layer, j, mode="plain", extra=()):
    n_tiles = BATCH * SEQ // TM
    in_specs = [_row_spec(D_MODEL)]
    args = [x]
    scratch = []
    if mode == "odd":
        fa, fb, perm, woa, wob = extra
        in_specs += [_res16_spec(fa.shape[-1]), _res16_spec(fb.shape[-1])]
        in_specs += [_const_spec(t.shape) for t in (perm, woa, wob)]
        args += [fa, fb, perm, woa, wob]
    elif mode == "even":
        u, cw, cb, lng, lnb, fb, woa, wob = extra
        nxt = pl.BlockSpec((TM, CONV_CH), lambda i: (jnp.minimum(i + 1, n_tiles - 1), 0))
        in_specs += [_row_spec(CONV_CH), nxt]
        in_specs += [_const_spec(t.shape) for t in (cw, cb, lng, lnb)]
        in_specs += [_row_spec(fb.shape[-1]), _const_spec(woa.shape), _const_spec(wob.shape)]
        args += [u, u, cw, cb, lng, lnb, fb, woa, wob]
        scratch = [pltpu.VMEM((CONV_HALO + TM + SUBLANES, CONV_CH), F32),
                   pltpu.VMEM((2, TM, CONV_CH), BF16)]
    in_specs += [_const_spec(g.shape)] + [_stack_spec(w, layer, j) for w in ffn_w]
    args += [g, *ffn_w]
    return pl.pallas_call(
        functools.partial(_ffn_kernel, mode=mode),
        grid=(n_tiles,), in_specs=in_specs, out_specs=_row_spec(D_MODEL),
        out_shape=jax.ShapeDtypeStruct(x.shape, F32), scratch_shapes=scratch,
        compiler_params=_params(1), name="ffn_" + mode,
    )(*args)


def _even_in_kernel(x_ref, g_ref, w_ref, bd_ref, gq_ref, gk_ref, u_ref, q_ref, k_ref, v_ref):
    hn = _rms(x_ref[...], g_ref[...]).astype(BF16)
    proj = _dot(hn, w_ref[...])
    a_val = proj[:, :CONV_CH]
    a_gate = proj[:, CONV_CH:2 * CONV_CH]
    u_ref[...] = a_val * _sigmoid(a_gate)
    qkv = proj[:, 2 * CONV_CH:]
    ss = _head_sumsq(qkv, bd_ref)
    q = qkv[:, :512]
    k = qkv[:, 512:640]
    q_ref[...] = (_head_rms(q, ss[:, :512], gq_ref[...]) * QK_SCALE).astype(BF16)
    k_ref[...] = _head_rms(k, ss[:, 512:640], gk_ref[...]).astype(BF16)
    v_ref[...] = qkv[:, 640:].astype(BF16)


def _even_in_call(x, g, w_in, bd, gq, gk):
    n = x.shape[0]
    row = lambda width: pl.BlockSpec((TM, width), lambda i: (i, 0))
    return pl.pallas_call(
        _even_in_kernel, grid=(n // TM,),
        in_specs=[row(D_MODEL), _const_spec(g.shape), _const_spec(w_in.shape),
                  _const_spec(bd.shape), _const_spec(gq.shape), _const_spec(gk.shape)],
        out_specs=[row(512), row(512), row(128), row(128)],
        out_shape=[jax.ShapeDtypeStruct((n, 512), F32), jax.ShapeDtypeStruct((n, 512), BF16),
                   jax.ShapeDtypeStruct((n, 128), BF16), jax.ShapeDtypeStruct((n, 128), BF16)],
        compiler_params=_params(1), name="even_in",
    )(x, g, w_in, bd, gq, gk)


def _pv_operand(vv):
    lo = lax.broadcasted_iota(jnp.int32, vv.shape, 1) < HEAD_DIM
    zero = jnp.zeros_like(vv)
    ones_lo = jnp.where(lo, 1.0, 0.0)
    top = jnp.concatenate([jnp.where(lo, vv, zero), ones_lo.astype(BF16)], axis=-1)
    bot = jnp.concatenate([jnp.where(lo, zero, vv), (1.0 - ones_lo).astype(BF16)], axis=-1)
    return jnp.concatenate([top, bot], axis=0)


def _attend(s, bias, vv, exact, sink=None):
    nh = s.shape[0] // BLOCK
    s3 = s.reshape(nh, BLOCK, s.shape[1]) + bias[None]
    lo_q = lax.broadcasted_iota(jnp.int32, (BLOCK, LANES), 1) < HEAD_DIM
    m = None
    if exact:
        m = jnp.max(s3, axis=-1, keepdims=True)
        if sink is not None:
            m = jnp.maximum(m, sink)
        s3 = s3 - m
        if sink is not None:
            sink = jnp.exp(sink - m)
    p = jnp.exp(s3)
    vext = _pv_operand(vv)
    outs = []
    for pr in range(nh // 2):
        pcat = jnp.concatenate([p[2 * pr], p[2 * pr + 1]], axis=-1).astype(BF16)
        res = _dot(pcat, vext)
        if sink is not None:
            term = jnp.where(lo_q, sink[2 * pr], sink[2 * pr + 1]) if exact else sink[pr]
            res = jnp.concatenate([res[:, :LANES], res[:, LANES:] + term], axis=-1)
        m_pair = jnp.where(lo_q, m[2 * pr], m[2 * pr + 1]) if exact else None
        outs.append((res, m_pair))
    return outs


def _stack_pairs(q_pairs):
    lo = lax.broadcasted_iota(jnp.int32, (BLOCK, LANES), 1) < HEAD_DIM
    parts = []
    for q in q_pairs:
        zero = jnp.zeros_like(q)
        parts += [jnp.where(lo, q, zero), jnp.where(lo, zero, q)]
    return jnp.concatenate(parts, axis=0)


def _even_mix_kernel(q_ref, k_ref, kh_ref, v_ref, vh_ref, sink_ref, bias_ref, fb_ref,
                     kcat, vcat, *, exact):
    first = pl.program_id(1) == 0
    kcat[:BLOCK] = kh_ref[0]
    kcat[BLOCK:] = k_ref[0]
    vcat[:BLOCK] = vh_ref[0]
    vcat[BLOCK:] = v_ref[0]
    sink = sink_ref[...]

    def attn_block(i, carry):
        base = pl.multiple_of(i * BLOCK, BLOCK)
        kk = kcat[pl.ds(base, 2 * BLOCK), :]
        vv = vcat[pl.ds(base, 2 * BLOCK), :]
        qs = _stack_pairs([q_ref[0, pl.ds(base, BLOCK), pr * LANES:(pr + 1) * LANES]
                           for pr in range(SWA_Q_HEADS // 2)])
        bias = bias_ref[jnp.where(jnp.logical_and(first, i == 0), 1, 0)]
        s = _dot_nt(qs, kk)
        for pr, (res, _) in enumerate(_attend(s, bias, vv, exact, sink)):
            out = res[:, :LANES] / res[:, LANES:]
            fb_ref[0, pl.ds(base, BLOCK), pr * LANES:(pr + 1) * LANES] = out.astype(BF16)
        return carry

    lax.fori_loop(0, EV_CHUNK // BLOCK, attn_block, 0, unroll=2)


def _even_mix_call(q, k, v, sink, bias, exact):
    nchunk = SEQ // EV_CHUNK
    cur = lambda width: pl.BlockSpec((1, EV_CHUNK, width), lambda b, c: (b, c, 0))
    halo = pl.BlockSpec((1, BLOCK, LANES),
                        lambda b, c: (b, jnp.maximum(c * (EV_CHUNK // BLOCK) - 1, 0), 0))
    return pl.pallas_call(
        functools.partial(_even_mix_kernel, exact=exact), grid=(BATCH, nchunk),
        in_specs=[cur(512), cur(128), halo, cur(128), halo, _const_spec(sink.shape),
                  _const_spec(bias.shape)],
        out_specs=cur(512),
        out_shape=jax.ShapeDtypeStruct((BATCH, SEQ, 512), BF16),
        scratch_shapes=[pltpu.VMEM((BLOCK + EV_CHUNK, LANES), BF16),
                        pltpu.VMEM((BLOCK + EV_CHUNK, LANES), BF16)],
        compiler_params=_params(2), name="even_mix_exact" if exact else "even_mix",
    )(q, k, k, v, v, sink, bias)


def _odd_in_kernel(x_ref, g_ref, perm_ref, w_ref, bd_ref, gq_ref, gk_ref,
                   q_ref, k_ref, v_ref, u_ref):
    hn = _rms(x_ref[...], g_ref[...]).astype(BF16)
    hn = _permute_rows(perm_ref, hn)
    proj = _dot(hn, w_ref[...])
    qk = proj[:, :1024]
    ss = _head_sumsq(qk, bd_ref)
    q = _head_rms(qk[:, :512], ss[:, :512], gq_ref[...]) * QK_SCALE
    k = _head_rms(qk[:, 512:], ss[:, 512:], gk_ref[...])
    v = proj[:, 1024:1536]
    u = proj[:, 1536:]
    rows = TM // N_RES
    for ref, val in ((q_ref, q), (k_ref, k), (v_ref, v), (u_ref, u)):
        for r in range(N_RES):
            ref[0, r] = val[r * rows:(r + 1) * rows]


def _odd_in_call(x, g, perm, w_in, bd, gq, gk):
    shp = jax.ShapeDtypeStruct((BATCH, N_RES, RES_ROWS, 512), F32)
    return pl.pallas_call(
        _odd_in_kernel, grid=(BATCH * SEQ // TM,),
        in_specs=[_row_spec(D_MODEL), _const_spec(g.shape), _const_spec(perm.shape),
                  _const_spec(w_in.shape), _const_spec(bd.shape), _const_spec(gq.shape),
                  _const_spec(gk.shape)],
        out_specs=[_res16_spec(512)] * 4,
        out_shape=[shp] * 4,
        compiler_params=_params(1), name="odd_in",
    )(x, g, perm, w_in, bd, gq, gk)


def _dil_attn_kernel(q_ref, k_ref, v_ref, bias_ref, o_ref, st, *m_scratch, exact):
    m_s = m_scratch[0] if exact else None

    def lanes(pi):
        return slice(pi * LANES, (pi + 1) * LANES)

    def gather(ref, slabs, start, rows, pi):
        parts = [ref[0, s, pl.ds(start, rows), lanes(pi)] for s in slabs]
        return jnp.concatenate(parts, axis=0).astype(BF16)

    def block(q, kk, vv, bias):
        return _attend(_dot_nt(_stack_pairs([q]), kk), bias, vv, exact)[0]

    def put(pi, slab, rows, res, m, init):
        idx = (pi, slab, rows, slice(None))
        if init:
            st[idx] = res
            if exact:
                m_s[idx] = m
        elif not exact:
            st[idx] += res
        else:
            m_old = m_s[idx]
            m_new = jnp.maximum(m_old, m)
            a = jnp.exp(m_old - m_new)
            b = jnp.exp(m - m_new)
            m_s[idx] = m_new
            st[idx] = (jnp.concatenate([a, a], axis=-1) * st[idx]
                       + jnp.concatenate([b, b], axis=-1) * res)

    def put_chunks(pi, slabs, start, rows, res, m):
        for c, s in enumerate(slabs):
            part = slice(c * rows, (c + 1) * rows)
            put(pi, s, pl.ds(start, rows), res[part], m[part] if exact else None, False)

    def branch16(i, carry):
        bias = bias_ref[0, 0]
        for r in (DIL16_SLABS_PER_TRIP * i + dr for dr in range(DIL16_SLABS_PER_TRIP)):
            for pi in range(DIL_STEP_PAIRS):
                q, k, v = (ref[0, r, :, lanes(pi)].astype(BF16) for ref in (q_ref, k_ref, v_ref))
                res, m = block(q[:BLOCK], k[:BLOCK], v[:BLOCK], bias[:, BLOCK:])
                put(pi, r, slice(0, BLOCK), res, m, True)
                res, m = block(q[BLOCK:], k, v, bias)
                put(pi, r, slice(BLOCK, 2 * BLOCK), res, m, True)
        return carry

    lax.fori_loop(0, N_RES // DIL16_SLABS_PER_TRIP, branch16, 0)

    def strided_block(bi, nslab, r0, n, first_possible):
        rows = BLOCK // nslab
        cur = pl.multiple_of(n * rows, rows)
        prev = pl.multiple_of(jnp.maximum(n - 1, 0) * rows, rows)
        slabs = [r0 + (N_RES // nslab) * c for c in range(nslab)]
        bias = bias_ref[bi, jnp.where(n == 0, 1, 0)] if first_possible else bias_ref[bi, 0]
        for pi in range(DIL_STEP_PAIRS):
            kk = jnp.concatenate([gather(k_ref, slabs, prev, rows, pi),
                                  gather(k_ref, slabs, cur, rows, pi)], axis=0)
            vv = jnp.concatenate([gather(v_ref, slabs, prev, rows, pi),
                                  gather(v_ref, slabs, cur, rows, pi)], axis=0)
            res, m = block(gather(q_ref, slabs, cur, rows, pi), kk, vv, bias)
            put_chunks(pi, slabs, cur, rows, res, m)

    def branch4(i, carry):
        for dn in range(DIL4_BLOCKS_PER_TRIP):
            for r0 in range(4):
                strided_block(1, 4, r0, DIL4_BLOCKS_PER_TRIP * i + dn, dn == 0)
        return carry

    lax.fori_loop(0, RES_ROWS // (BLOCK // 4) // DIL4_BLOCKS_PER_TRIP, branch4, 0)

    def branch1(i, carry):
        for dn in range(DIL1_BLOCKS_PER_TRIP):
            strided_block(2, N_RES, 0, DIL1_BLOCKS_PER_TRIP * i + dn, dn == 0)
        return carry

    lax.fori_loop(0, RES_ROWS // (BLOCK // N_RES) // DIL1_BLOCKS_PER_TRIP, branch1, 0)

    for pi in range(DIL_STEP_PAIRS):
        o_ref[0, :, :, lanes(pi)] = (st[pi, :, :, :LANES] / st[pi, :, :, LANES:]).astype(BF16)


def _dil_attn_call(q, k, v, bias, exact):
    width = DIL_STEP_PAIRS * LANES
    blk = lambda: pl.BlockSpec((1, N_RES, RES_ROWS, width), lambda b, p: (b, 0, 0, p))
    scratch = [pltpu.VMEM((DIL_STEP_PAIRS, N_RES, RES_ROWS, 2 * LANES), F32)]
    if exact:
        scratch.append(pltpu.VMEM((DIL_STEP_PAIRS, N_RES, RES_ROWS, LANES), F32))
    return pl.pallas_call(
        functools.partial(_dil_attn_kernel, exact=exact),
        grid=(BATCH, DIL_HEADS // 2 // DIL_STEP_PAIRS),
        in_specs=[blk(), blk(), blk(), _const_spec(bias.shape)],
        out_specs=blk(),
        out_shape=jax.ShapeDtypeStruct((BATCH, N_RES, RES_ROWS, 512), BF16),
        scratch_shapes=scratch,
        compiler_params=_params(2), name="dil_attn_exact" if exact else "dil_attn",
    )(q, k, v, bias)


def _pool_kernel(u_ref, w_ref, sc_ref, o_ref):
    row0 = lax.broadcasted_iota(jnp.int32, (RES_ROWS, POOL_GROUP), 0) == 0

    def shift(x):
        return jnp.where(row0, 0.0, pltpu.roll(x, 1, axis=0))

    for gi, w in enumerate(POOL_SIZES):
        lanes = slice(gi * POOL_GROUP, (gi + 1) * POOL_GROUP)
        slab = [u_ref[0, r, :, lanes] for r in range(N_RES)]
        pre = [slab[0]]
        for r in range(1, N_RES):
            pre.append(pre[-1] + slab[r])
        tot_sh = shift(pre[-1])
        pooled = []
        for r in range(N_RES):
            if r >= w:
                win = pre[r] - pre[r - w]
            elif r == w - 1:
                win = pre[r]
            else:
                win = pre[r] + (tot_sh - shift(pre[r - w + N_RES]))
            inv = jnp.where(row0, 1.0 / min(r + 1, w), 1.0 / w)
            pooled.append((win * inv - slab[r]).astype(BF16))
        out = _dot(jnp.concatenate(pooled, axis=0), w_ref[gi]) * sc_ref[:, lanes]
        for r in range(N_RES):
            o_ref[0, r, :, lanes] = out[r * RES_ROWS:(r + 1) * RES_ROWS].astype(BF16)


def _pool_call(u, w, sc):
    blk = lambda: pl.BlockSpec((1, N_RES, RES_ROWS, POOL_CH), lambda b: (b, 0, 0, 0))
    return pl.pallas_call(
        _pool_kernel, grid=(BATCH,),
        in_specs=[blk(), _const_spec(w.shape), _const_spec(sc.shape)],
        out_specs=blk(),
        out_shape=jax.ShapeDtypeStruct((BATCH, N_RES, RES_ROWS, POOL_CH), BF16),
        compiler_params=_params(1), name="pool",
    )(u, w, sc)


def _band_bias(nslab, max_dist):
    rows = BLOCK // nslab
    s = np.arange(BLOCK)
    pos = nslab * (s % rows) + s // rows
    dist = pos[:, None] - np.concatenate([pos - BLOCK, pos])[None, :]
    gen = np.where((dist >= 0) & (dist <= max_dist), 0.0, NEG_INF).astype(np.float32)
    fst = gen.copy()
    fst[:, :BLOCK] = NEG_INF
    return np.stack([gen, fst])


def _block_diag_ones():
    i = np.arange(MXU_DIM) // HEAD_DIM
    return jnp.asarray((i[:, None] == i[None, :]).astype(np.float32), dtype=BF16)


def _cast_kernel(w_ref, o_ref):
    o_ref[...] = w_ref[...].astype(BF16)


def _cast_stack(w):
    depth, two, rows, cols = w.shape
    blk = pl.BlockSpec((None, None, rows // 2, cols), lambda l, j, h: (l, j, h, 0))
    return pl.pallas_call(
        _cast_kernel, grid=(depth, two, 2), in_specs=[blk], out_specs=blk,
        out_shape=jax.ShapeDtypeStruct(w.shape, BF16),
        compiler_params=_params(3), name="cast_weights",
    )(w)


def _tile_permutation():
    rows = TM // N_RES
    dst = np.arange(TM)
    src = (dst % rows) * N_RES + dst // rows
    return jnp.asarray((src[:, None] == np.arange(TM)[None, :]).astype(np.float32), dtype=BF16)


def _tile_heads(g, n):
    return jnp.tile(g, n)[None, :]


def _qk_shift(gq, gk):
    return QK_BOUND * jnp.max(jnp.abs(gq)) * jnp.max(jnp.abs(gk))


_EV_HEAD_ORDER = np.array([h for p in range(4) for h in (p, 4 + p)])
_EV_Q_PERM = (_EV_HEAD_ORDER[:, None] * HEAD_DIM + np.arange(HEAD_DIM)[None, :]).reshape(-1)


def kernel(x, norm_g, ffn_w_gate, ffn_w_up, ffn_w_down, ev_w_in, ev_w_out, ev_conv_w, ev_conv_b,
           ev_ln_g, ev_ln_b, ev_q_norm_g, ev_k_norm_g, ev_sinks, od_w_in, od_w_out, od_q_norm_g,
           od_k_norm_g, od_pool_w, od_pool_scale):
    bd = _block_diag_ones()
    ev_bias = jnp.asarray(_band_bias(1, SWA_WINDOW - 1))
    od_bias = jnp.asarray(np.stack([_band_bias(1, BLOCK), _band_bias(4, BLOCK),
                                    _band_bias(16, BLOCK)]))
    to_res16 = _tile_permutation()
    from_res16 = to_res16.T
    ffn_w = (_cast_stack(ffn_w_gate), _cast_stack(ffn_w_up), _cast_stack(ffn_w_down))
    x = x.reshape(BATCH * SEQ, D_MODEL)
    for layer in range(DEPTH):
        g = norm_g[layer]
        i = layer // 2
        x = _ffn_call(x, g[0:1], ffn_w, layer, 0)
        if layer % 2 == 0:
            w_in = jnp.concatenate([ev_w_in[i][:, :1024], ev_w_in[i][:, 1024 + _EV_Q_PERM],
                                    ev_w_in[i][:, 1536:]], axis=1).astype(BF16)
            u, q, k, v = _even_in_call(x, g[1:2], w_in, bd,
                                       _tile_heads(ev_q_norm_g[i], 8), _tile_heads(ev_k_norm_g[i], 2))
            sink = ev_sinks[i][_EV_HEAD_ORDER]
            r3 = lambda t: t.reshape(BATCH, SEQ, t.shape[-1])
            shift = _qk_shift(ev_q_norm_g[i], ev_k_norm_g[i])
            use_shift = jnp.logical_and(shift <= MAX_CONST_SHIFT,
                                        jnp.max(sink) - shift <= MAX_SINK_OVER_SHIFT)
            sink_terms = jnp.repeat(jnp.exp(sink - shift), HEAD_DIM).reshape(SWA_Q_HEADS // 2, 1, LANES)
            mix = lambda sk, bs, exact: _even_mix_call(r3(q), r3(k), r3(v), sk, bs, exact)
            fb = lax.cond(use_shift,
                          lambda: mix(sink_terms, ev_bias - shift, False),
                          lambda: mix(sink.reshape(SWA_Q_HEADS, 1, 1), ev_bias, True))
            w_out = ev_w_out[i].astype(BF16)
            extra = (u, jnp.pad(ev_conv_w[i], ((0, 1), (0, 0))), ev_conv_b[i][None],
                     ev_ln_g[i][None], ev_ln_b[i][None], fb.reshape(-1, 512),
                     w_out[:512], w_out[512 + _EV_Q_PERM])
            x = _ffn_call(x, g[2:3], ffn_w, layer, 1, "even", extra)
        else:
            q, k, v, u = _odd_in_call(x, g[1:2], to_res16, od_w_in[i].astype(BF16), bd,
                                      _tile_heads(od_q_norm_g[i], 8), _tile_heads(od_k_norm_g[i], 8))
            shift = _qk_shift(od_q_norm_g[i], od_k_norm_g[i])
            att = lax.cond(shift <= MAX_CONST_SHIFT,
                           lambda: _dil_attn_call(q, k, v, od_bias - shift, False),
                           lambda: _dil_attn_call(q, k, v, od_bias, True))
            pool = _pool_call(u, od_pool_w[i].astype(BF16), od_pool_scale[i][None])
            w_out = od_w_out[i].astype(BF16)
            extra = (att, pool, from_res16, w_out[:512], w_out[512:])
            x = _ffn_call(x, g[2:3], ffn_w, layer, 1, "odd", extra)
    return x.reshape(BATCH, SEQ, D_MODEL)
```

```python
import functools

import jax
import jax.numpy as jnp
import numpy as np
from jax import lax
from jax.experimental import pallas as pl
from jax.experimental.pallas import tpu as pltpu

F32 = jnp.float32
BF16 = jnp.bfloat16

D_MODEL = 1024
BATCH = 8
SEQ = 4096
DEPTH = 4
HEAD_DIM = 64
D_FF = 2752
EPS = 1e-6
NEG_INF = -1e30
BLOCK = 128
CONV_CH = 512
CONV_WIDTH = 31
SWA_Q_HEADS = 8
SWA_WINDOW = 128
EVEN_IN = 1792
DIL_HEADS = 8
POOL_CH = 512
POOL_SIZES = (2, 4, 8, 16)
POOL_GROUP = 128
ODD_IN = 2048
QK_SCALE = HEAD_DIM ** -0.5

LANES = 128
SUBLANES = 8
MXU_DIM = 256
N_RES = 16
RES_ROWS = SEQ // N_RES
TM = 512
TM_PLAIN = 1024
PLAIN_PART = 512
EV_CHUNK = 1024
CONV_HALO = 32
DIL_STEP_PAIRS = 2
DIL16_SLABS_PER_TRIP = 8
DIL4_BLOCKS_PER_TRIP = 4
DIL1_BLOCKS_PER_TRIP = 16
QK_BOUND = HEAD_DIM * QK_SCALE * 1.03
MAX_CONST_SHIFT = 20.0
MAX_SINK_OVER_SHIFT = 60.0
VMEM_LIMIT = 56 * 1024 * 1024


def _params(n_axes):
    return pltpu.CompilerParams(dimension_semantics=("arbitrary",) * n_axes,
                                vmem_limit_bytes=VMEM_LIMIT)


def _const_spec(shape):
    nd = len(shape)
    return pl.BlockSpec(shape, lambda *_: (0,) * nd, pipeline_mode=pl.Buffered(1))


def _rms(x, g):
    return x * lax.rsqrt(jnp.mean(x * x, axis=-1, keepdims=True) + EPS) * g


def _sigmoid(x, one=1.0):
    return 1.0 / (one + jnp.exp(-x))


def _dot(a, b):
    return jnp.dot(a, b, preferred_element_type=F32)


def _dot_nt(a, b):
    return lax.dot_general(a, b, (((1,), (1,)), ((), ())), preferred_element_type=F32)


def _head_sumsq(x, bd_ref):
    sq = (x * x).astype(BF16)
    bd = bd_ref[...]
    outs = [_dot(sq[:, c * MXU_DIM:(c + 1) * MXU_DIM], bd) for c in range(x.shape[1] // MXU_DIM)]
    return outs[0] if len(outs) == 1 else jnp.concatenate(outs, axis=-1)


def _head_rms(x, ss, g):
    return x * lax.rsqrt(ss * (1.0 / HEAD_DIM) + EPS) * g


def _permute_rows(perm_ref, val):
    return _dot(perm_ref[...], val).astype(BF16)


def _conv_fill(ucat, halo, u_ref):
    ucat[:CONV_HALO] = halo
    ucat[CONV_HALO:CONV_HALO + TM] = u_ref[...]
    ucat[CONV_HALO + TM:] = jnp.zeros((SUBLANES, CONV_CH), F32)


def _conv_block(blk, ucat, cw_ref, cb_ref, lng_ref, lnb_ref, out_ref):
    off = CONV_HALO - (CONV_WIDTH - 1)
    base = blk * BLOCK
    wins = [ucat[base + SUBLANES * eh:base + SUBLANES * eh + BLOCK + SUBLANES, :]
            for eh in range(CONV_HALO // SUBLANES + 1)]
    acc = jnp.broadcast_to(cb_ref[...], (BLOCK, CONV_CH))
    for el in range(SUBLANES):
        part = None
        for eh, win in enumerate(wins):
            j = SUBLANES * eh + el - off
            if 0 <= j < CONV_WIDTH:
                term = cw_ref[j:j + 1, :] * win
                part = term if part is None else part + term
        acc = acc + part[el:el + BLOCK]
    mu = jnp.mean(acc, axis=-1, keepdims=True)
    d = acc - mu
    var = jnp.mean(d * d, axis=-1, keepdims=True)
    y = d * lax.rsqrt(var + EPS) * lng_ref[...] + lnb_ref[...]
    out = y * _sigmoid(y)
    out_ref[base:base + BLOCK, :] = out.astype(BF16)
    return _dependent_zero(out[:SUBLANES, :LANES])


def _dependent_zero(t):
    bits = lax.bitcast_convert_type(t, jnp.int32)
    zero = lax.shift_right_logical(lax.shift_right_logical(bits, 16), 16)
    return jnp.max(zero.astype(F32), axis=(0, 1), keepdims=True)


def _ffn_kernel(*refs, mode):
    if mode == "even":
        (x_ref, u_ref, un_ref, cw_ref, cb_ref, lng_ref, lnb_ref, fb_ref, woa_ref, wob_ref,
         g_ref, wg_ref, wu_ref, wd_ref, o_ref, ucat, fa_scr) = refs
    elif mode == "odd":
        x_ref, fa_ref, fb_ref, perm_ref, woa_ref, wob_ref, g_ref, wg_ref, wu_ref, wd_ref, o_ref = refs
    else:
        x_ref, g_ref, wg_ref, wu_ref, wd_ref, o_ref = refs
    if mode == "plain":
        for part in range(TM_PLAIN // PLAIN_PART):
            rows = slice(part * PLAIN_PART, (part + 1) * PLAIN_PART)
            xp = x_ref[rows, :]
            xn = _rms(xp, g_ref[...]).astype(BF16)
            h = _dot(xn, wg_ref[...])
            u = _dot(xn, wu_ref[...])
            a = (h * _sigmoid(h) * u).astype(BF16)
            o_ref[rows, :] = xp + 0.5 * _dot(a, wd_ref[...])
        return
    x = x_ref[...]
    conv_next = lambda blk: 0.0
    if mode == "even":
        i = pl.program_id(0)
        conv = functools.partial(_conv_block, ucat=ucat, cw_ref=cw_ref, cb_ref=cb_ref,
                                 lng_ref=lng_ref, lnb_ref=lnb_ref)

        @pl.when(i == 0)
        def _():
            _conv_fill(ucat, jnp.zeros((CONV_HALO, CONV_CH), F32), u_ref)
            for blk in range(TM // BLOCK):
                conv(blk, out_ref=fa_scr.at[0])

        seq_start = lax.rem(i + 1, SEQ // TM) == 0
        _conv_fill(ucat, jnp.where(seq_start, 0.0, u_ref[TM - CONV_HALO:, :]), un_ref)
        conv_next = functools.partial(conv, out_ref=fa_scr.at[lax.rem(i + 1, 2)])
        x = x + _dot(fa_scr[lax.rem(i, 2)], woa_ref[...]) + _dot(fb_ref[...], wob_ref[...])
    elif mode == "odd":
        feats = [_permute_rows(perm_ref, f_ref[0].reshape(TM, f_ref.shape[-1]))
                 for f_ref in (fa_ref, fb_ref)]
        x = x + _dot(feats[0], woa_ref[...]) + _dot(feats[1], wob_ref[...])
    xn = _rms(x, g_ref[...]).astype(BF16)
    h = _dot(xn, wg_ref[...])
    u = _dot(xn, wu_ref[...])
    one = 1.0 + (conv_next(0) + conv_next(1))
    a = (h * _sigmoid(h, one) * u).astype(BF16)
    y = _dot(a, wd_ref[...])
    half = 0.5 + (conv_next(2) + conv_next(3))
    o_ref[...] = x + half * y


def _row_spec(width, rows=TM):
    return pl.BlockSpec((rows, width), lambda i: (i, 0))


def _res16_spec(width):
    tiles = SEQ // TM
    return pl.BlockSpec((1, N_RES, TM // N_RES, width), lambda i: (i // tiles, 0, i % tiles, 0))


def _stack_spec(w, layer, j):
    return pl.BlockSpec((None, None) + w.shape[2:], lambda i: (layer, j, 0, 0),
                        pipeline_mode=pl.Buffered(1))


def _ffn_call(x, g, ffn_w, layer, j, mode="plain", extra=()):
    rows = TM_PLAIN if mode == "plain" else TM
    n_tiles = BATCH * SEQ // rows
    in_specs = [_row_spec(D_MODEL, rows)]
    args = [x]
    scratch = []
    if mode == "odd":
        fa, fb, perm, woa, wob = extra
        in_specs += [_res16_spec(fa.shape[-1]), _res16_spec(fb.shape[-1])]
        in_specs += [_const_spec(t.shape) for t in (perm, woa, wob)]
        args += [fa, fb, perm, woa, wob]
    elif mode == "even":
        u, cw, cb, lng, lnb, fb, woa, wob = extra
        nxt = pl.BlockSpec((TM, CONV_CH), lambda i: (jnp.minimum(i + 1, n_tiles - 1), 0))
        in_specs += [_row_spec(CONV_CH), nxt]
        in_specs += [_const_spec(t.shape) for t in (cw, cb, lng, lnb)]
        in_specs += [_row_spec(fb.shape[-1]), _const_spec(woa.shape), _const_spec(wob.shape)]
        args += [u, u, cw, cb, lng, lnb, fb, woa, wob]
        scratch = [pltpu.VMEM((CONV_HALO + TM + SUBLANES, CONV_CH), F32),
                   pltpu.VMEM((2, TM, CONV_CH), BF16)]
    in_specs += [_const_spec(g.shape)] + [_stack_spec(w, layer, j) for w in ffn_w]
    args += [g, *ffn_w]
    return pl.pallas_call(
        functools.partial(_ffn_kernel, mode=mode),
        grid=(n_tiles,), in_specs=in_specs, out_specs=_row_spec(D_MODEL, rows),
        out_shape=jax.ShapeDtypeStruct(x.shape, F32), scratch_shapes=scratch,
        compiler_params=_params(1), name="ffn_" + mode,
    )(*args)


def _even_in_kernel(x_ref, g_ref, w_ref, bd_ref, gq_ref, gk_ref, u_ref, q_ref, k_ref, v_ref):
    hn = _rms(x_ref[...], g_ref[...]).astype(BF16)
    proj = _dot(hn, w_ref[...])
    a_val = proj[:, :CONV_CH]
    a_gate = proj[:, CONV_CH:2 * CONV_CH]
    u_ref[...] = a_val * _sigmoid(a_gate)
    qkv = proj[:, 2 * CONV_CH:]
    ss = _head_sumsq(qkv, bd_ref)
    q = qkv[:, :512]
    k = qkv[:, 512:640]
    q_ref[...] = (_head_rms(q, ss[:, :512], gq_ref[...]) * QK_SCALE).astype(BF16)
    k_ref[...] = _head_rms(k, ss[:, 512:640], gk_ref[...]).astype(BF16)
    v_ref[...] = qkv[:, 640:].astype(BF16)


def _even_in_call(x, g, w_in, bd, gq, gk):
    n = x.shape[0]
    row = lambda width: pl.BlockSpec((TM, width), lambda i: (i, 0))
    return pl.pallas_call(
        _even_in_kernel, grid=(n // TM,),
        in_specs=[row(D_MODEL), _const_spec(g.shape), _const_spec(w_in.shape),
                  _const_spec(bd.shape), _const_spec(gq.shape), _const_spec(gk.shape)],
        out_specs=[row(512), row(512), row(128), row(128)],
        out_shape=[jax.ShapeDtypeStruct((n, 512), F32), jax.ShapeDtypeStruct((n, 512), BF16),
                   jax.ShapeDtypeStruct((n, 128), BF16), jax.ShapeDtypeStruct((n, 128), BF16)],
        compiler_params=_params(1), name="even_in",
    )(x, g, w_in, bd, gq, gk)


def _pv_operand(vv):
    lo = lax.broadcasted_iota(jnp.int32, vv.shape, 1) < HEAD_DIM
    zero = jnp.zeros_like(vv)
    ones_lo = jnp.where(lo, 1.0, 0.0)
    top = jnp.concatenate([jnp.where(lo, vv, zero), ones_lo.astype(BF16)], axis=-1)
    bot = jnp.concatenate([jnp.where(lo, zero, vv), (1.0 - ones_lo).astype(BF16)], axis=-1)
    return jnp.concatenate([top, bot], axis=0)


def _attend(s, bias, vv, exact, sink=None):
    nh = s.shape[0] // BLOCK
    s3 = s.reshape(nh, BLOCK, s.shape[1]) + bias[None]
    lo_q = lax.broadcasted_iota(jnp.int32, (BLOCK, LANES), 1) < HEAD_DIM
    m = None
    if exact:
        m = jnp.max(s3, axis=-1, keepdims=True)
        if sink is not None:
            m = jnp.maximum(m, sink)
        s3 = s3 - m
        if sink is not None:
            sink = jnp.exp(sink - m)
    p = jnp.exp(s3)
    vext = _pv_operand(vv)
    outs = []
    for pr in range(nh // 2):
        pcat = jnp.concatenate([p[2 * pr], p[2 * pr + 1]], axis=-1).astype(BF16)
        res = _dot(pcat, vext)
        if sink is not None:
            term = jnp.where(lo_q, sink[2 * pr], sink[2 * pr + 1]) if exact else sink[pr]
            res = jnp.concatenate([res[:, :LANES], res[:, LANES:] + term], axis=-1)
        m_pair = jnp.where(lo_q, m[2 * pr], m[2 * pr + 1]) if exact else None
        outs.append((res, m_pair))
    return outs


def _stack_pairs(q_pairs):
    lo = lax.broadcasted_iota(jnp.int32, (BLOCK, LANES), 1) < HEAD_DIM
    parts = []
    for q in q_pairs:
        zero = jnp.zeros_like(q)
        parts += [jnp.where(lo, q, zero), jnp.where(lo, zero, q)]
    return jnp.concatenate(parts, axis=0)


def _even_mix_kernel(q_ref, k_ref, kh_ref, v_ref, vh_ref, sink_ref, bias_ref, fb_ref,
                     kcat, vcat, *, exact):
    first = pl.program_id(1) == 0
    kcat[:BLOCK] = kh_ref[0]
    kcat[BLOCK:] = k_ref[0]
    vcat[:BLOCK] = vh_ref[0]
    vcat[BLOCK:] = v_ref[0]
    sink = sink_ref[...]

    def attn_block(i, carry):
        base = pl.multiple_of(i * BLOCK, BLOCK)
        kk = kcat[pl.ds(base, 2 * BLOCK), :]
        vv = vcat[pl.ds(base, 2 * BLOCK), :]
        qs = _stack_pairs([q_ref[0, pl.ds(base, BLOCK), pr * LANES:(pr + 1) * LANES]
                           for pr in range(SWA_Q_HEADS // 2)])
        bias = bias_ref[jnp.where(jnp.logical_and(first, i == 0), 1, 0)]
        s = _dot_nt(qs, kk)
        for pr, (res, _) in enumerate(_attend(s, bias, vv, exact, sink)):
            out = res[:, :LANES] / res[:, LANES:]
            fb_ref[0, pl.ds(base, BLOCK), pr * LANES:(pr + 1) * LANES] = out.astype(BF16)
        return carry

    lax.fori_loop(0, EV_CHUNK // BLOCK, attn_block, 0, unroll=2)


def _even_mix_call(q, k, v, sink, bias, exact):
    nchunk = SEQ // EV_CHUNK
    cur = lambda width: pl.BlockSpec((1, EV_CHUNK, width), lambda b, c: (b, c, 0))
    halo = pl.BlockSpec((1, BLOCK, LANES),
                        lambda b, c: (b, jnp.maximum(c * (EV_CHUNK // BLOCK) - 1, 0), 0))
    return pl.pallas_call(
        functools.partial(_even_mix_kernel, exact=exact), grid=(BATCH, nchunk),
        in_specs=[cur(512), cur(128), halo, cur(128), halo, _const_spec(sink.shape),
                  _const_spec(bias.shape)],
        out_specs=cur(512),
        out_shape=jax.ShapeDtypeStruct((BATCH, SEQ, 512), BF16),
        scratch_shapes=[pltpu.VMEM((BLOCK + EV_CHUNK, LANES), BF16),
                        pltpu.VMEM((BLOCK + EV_CHUNK, LANES), BF16)],
        compiler_params=_params(2), name="even_mix_exact" if exact else "even_mix",
    )(q, k, k, v, v, sink, bias)


def _odd_in_kernel(x_ref, g_ref, perm_ref, w_ref, bd_ref, gq_ref, gk_ref,
                   q_ref, k_ref, v_ref, u_ref):
    hn = _rms(x_ref[...], g_ref[...]).astype(BF16)
    hn = _permute_rows(perm_ref, hn)
    proj = _dot(hn, w_ref[...])
    qk = proj[:, :1024]
    ss = _head_sumsq(qk, bd_ref)
    q = _head_rms(qk[:, :512], ss[:, :512], gq_ref[...]) * QK_SCALE
    k = _head_rms(qk[:, 512:], ss[:, 512:], gk_ref[...])
    v = proj[:, 1024:1536]
    u = proj[:, 1536:]
    rows = TM // N_RES
    for ref, val in ((q_ref, q), (k_ref, k), (v_ref, v), (u_ref, u)):
        for r in range(N_RES):
            ref[0, r] = val[r * rows:(r + 1) * rows]


def _odd_in_call(x, g, perm, w_in, bd, gq, gk):
    shp = jax.ShapeDtypeStruct((BATCH, N_RES, RES_ROWS, 512), F32)
    return pl.pallas_call(
        _odd_in_kernel, grid=(BATCH * SEQ // TM,),
        in_specs=[_row_spec(D_MODEL), _const_spec(g.shape), _const_spec(perm.shape),
                  _const_spec(w_in.shape), _const_spec(bd.shape), _const_spec(gq.shape),
                  _const_spec(gk.shape)],
        out_specs=[_res16_spec(512)] * 4,
        out_shape=[shp] * 4,
        compiler_params=_params(1), name="odd_in",
    )(x, g, perm, w_in, bd, gq, gk)


def _dil_attn_kernel(q_ref, k_ref, v_ref, bias_ref, o_ref, st, *m_scratch, exact):
    m_s = m_scratch[0] if exact else None

    def lanes(pi):
        return slice(pi * LANES, (pi + 1) * LANES)

    def gather(ref, slabs, start, rows, pi):
        parts = [ref[0, s, pl.ds(start, rows), lanes(pi)] for s in slabs]
        return jnp.concatenate(parts, axis=0).astype(BF16)

    def block(q, kk, vv, bias):
        return _attend(_dot_nt(_stack_pairs([q]), kk), bias, vv, exact)[0]

    def put(pi, slab, rows, res, m, init):
        idx = (pi, slab, rows, slice(None))
        if init:
            st[idx] = res
            if exact:
                m_s[idx] = m
        elif not exact:
            st[idx] += res
        else:
            m_old = m_s[idx]
            m_new = jnp.maximum(m_old, m)
            a = jnp.exp(m_old - m_new)
            b = jnp.exp(m - m_new)
            m_s[idx] = m_new
            st[idx] = (jnp.concatenate([a, a], axis=-1) * st[idx]
                       + jnp.concatenate([b, b], axis=-1) * res)

    def put_chunks(pi, slabs, start, rows, res, m):
        for c, s in enumerate(slabs):
            part = slice(c * rows, (c + 1) * rows)
            put(pi, s, pl.ds(start, rows), res[part], m[part] if exact else None, False)

    def branch16(i, carry):
        bias = bias_ref[0, 0]
        for r in (DIL16_SLABS_PER_TRIP * i + dr for dr in range(DIL16_SLABS_PER_TRIP)):
            for pi in range(DIL_STEP_PAIRS):
                q, k, v = (ref[0, r, :, lanes(pi)].astype(BF16) for ref in (q_ref, k_ref, v_ref))
                res, m = block(q[:BLOCK], k[:BLOCK], v[:BLOCK], bias[:, BLOCK:])
                put(pi, r, slice(0, BLOCK), res, m, True)
                res, m = block(q[BLOCK:], k, v, bias)
                put(pi, r, slice(BLOCK, 2 * BLOCK), res, m, True)
        return carry

    lax.fori_loop(0, N_RES // DIL16_SLABS_PER_TRIP, branch16, 0)

    def strided_block(bi, nslab, r0, n, first_possible):
        rows = BLOCK // nslab
        cur = pl.multiple_of(n * rows, rows)
        prev = pl.multiple_of(jnp.maximum(n - 1, 0) * rows, rows)
        slabs = [r0 + (N_RES // nslab) * c for c in range(nslab)]
        bias = bias_ref[bi, jnp.where(n == 0, 1, 0)] if first_possible else bias_ref[bi, 0]
        for pi in range(DIL_STEP_PAIRS):
            kk = jnp.concatenate([gather(k_ref, slabs, prev, rows, pi),
                                  gather(k_ref, slabs, cur, rows, pi)], axis=0)
            vv = jnp.concatenate([gather(v_ref, slabs, prev, rows, pi),
                                  gather(v_ref, slabs, cur, rows, pi)], axis=0)
            res, m = block(gather(q_ref, slabs, cur, rows, pi), kk, vv, bias)
            put_chunks(pi, slabs, cur, rows, res, m)

    def branch4(i, carry):
        for dn in range(DIL4_BLOCKS_PER_TRIP):
            for r0 in range(4):
                strided_block(1, 4, r0, DIL4_BLOCKS_PER_TRIP * i + dn, dn == 0)
        return carry

    lax.fori_loop(0, RES_ROWS // (BLOCK // 4) // DIL4_BLOCKS_PER_TRIP, branch4, 0)

    def branch1(i, carry):
        for dn in range(DIL1_BLOCKS_PER_TRIP):
            strided_block(2, N_RES, 0, DIL1_BLOCKS_PER_TRIP * i + dn, dn == 0)
        return carry

    lax.fori_loop(0, RES_ROWS // (BLOCK // N_RES) // DIL1_BLOCKS_PER_TRIP, branch1, 0)

    for pi in range(DIL_STEP_PAIRS):
        o_ref[0, :, :, lanes(pi)] = (st[pi, :, :, :LANES] / st[pi, :, :, LANES:]).astype(BF16)


def _dil_attn_call(q, k, v, bias, exact):
    width = DIL_STEP_PAIRS * LANES
    blk = lambda: pl.BlockSpec((1, N_RES, RES_ROWS, width), lambda b, p: (b, 0, 0, p))
    scratch = [pltpu.VMEM((DIL_STEP_PAIRS, N_RES, RES_ROWS, 2 * LANES), F32)]
    if exact:
        scratch.append(pltpu.VMEM((DIL_STEP_PAIRS, N_RES, RES_ROWS, LANES), F32))
    return pl.pallas_call(
        functools.partial(_dil_attn_kernel, exact=exact),
        grid=(BATCH, DIL_HEADS // 2 // DIL_STEP_PAIRS),
        in_specs=[blk(), blk(), blk(), _const_spec(bias.shape)],
        out_specs=blk(),
        out_shape=jax.ShapeDtypeStruct((BATCH, N_RES, RES_ROWS, 512), BF16),
        scratch_shapes=scratch,
        compiler_params=_params(2), name="dil_attn_exact" if exact else "dil_attn",
    )(q, k, v, bias)


def _pool_kernel(u_ref, w_ref, sc_ref, o_ref):
    row0 = lax.broadcasted_iota(jnp.int32, (RES_ROWS, POOL_GROUP), 0) == 0

    def shift(x):
        return jnp.where(row0, 0.0, pltpu.roll(x, 1, axis=0))

    for gi, w in enumerate(POOL_SIZES):
        lanes = slice(gi * POOL_GROUP, (gi + 1) * POOL_GROUP)
        slab = [u_ref[0, r, :, lanes] for r in range(N_RES)]
        pre = [slab[0]]
        for r in range(1, N_RES):
            pre.append(pre[-1] + slab[r])
        tot_sh = shift(pre[-1])
        pooled = []
        for r in range(N_RES):
            if r >= w:
                win = pre[r] - pre[r - w]
            elif r == w - 1:
                win = pre[r]
            else:
                win = pre[r] + (tot_sh - shift(pre[r - w + N_RES]))
            inv = jnp.where(row0, 1.0 / min(r + 1, w), 1.0 / w)
            pooled.append((win * inv - slab[r]).astype(BF16))
        out = _dot(jnp.concatenate(pooled, axis=0), w_ref[gi]) * sc_ref[:, lanes]
        for r in range(N_RES):
            o_ref[0, r, :, lanes] = out[r * RES_ROWS:(r + 1) * RES_ROWS].astype(BF16)


def _pool_call(u, w, sc):
    blk = lambda: pl.BlockSpec((1, N_RES, RES_ROWS, POOL_CH), lambda b: (b, 0, 0, 0))
    return pl.pallas_call(
        _pool_kernel, grid=(BATCH,),
        in_specs=[blk(), _const_spec(w.shape), _const_spec(sc.shape)],
        out_specs=blk(),
        out_shape=jax.ShapeDtypeStruct((BATCH, N_RES, RES_ROWS, POOL_CH), BF16),
        compiler_params=_params(1), name="pool",
    )(u, w, sc)


def _band_bias(nslab, max_dist):
    rows = BLOCK // nslab
    s = np.arange(BLOCK)
    pos = nslab * (s % rows) + s // rows
    dist = pos[:, None] - np.concatenate([pos - BLOCK, pos])[None, :]
    gen = np.where((dist >= 0) & (dist <= max_dist), 0.0, NEG_INF).astype(np.float32)
    fst = gen.copy()
    fst[:, :BLOCK] = NEG_INF
    return np.stack([gen, fst])


def _block_diag_ones():
    i = np.arange(MXU_DIM) // HEAD_DIM
    return jnp.asarray((i[:, None] == i[None, :]).astype(np.float32), dtype=BF16)


def _tile_permutation():
    rows = TM // N_RES
    dst = np.arange(TM)
    src = (dst % rows) * N_RES + dst // rows
    return jnp.asarray((src[:, None] == np.arange(TM)[None, :]).astype(np.float32), dtype=BF16)


def _tile_heads(g, n):
    return jnp.tile(g, n)[None, :]


def _qk_shift(gq, gk):
    return QK_BOUND * jnp.max(jnp.abs(gq)) * jnp.max(jnp.abs(gk))


_EV_HEAD_ORDER = np.array([h for p in range(4) for h in (p, 4 + p)])
_EV_Q_PERM = (_EV_HEAD_ORDER[:, None] * HEAD_DIM + np.arange(HEAD_DIM)[None, :]).reshape(-1)


def kernel(x, norm_g, ffn_w_gate, ffn_w_up, ffn_w_down, ev_w_in, ev_w_out, ev_conv_w, ev_conv_b,
           ev_ln_g, ev_ln_b, ev_q_norm_g, ev_k_norm_g, ev_sinks, od_w_in, od_w_out, od_q_norm_g,
           od_k_norm_g, od_pool_w, od_pool_scale):
    bd = _block_diag_ones()
    ev_bias = jnp.asarray(_band_bias(1, SWA_WINDOW - 1))
    od_bias = jnp.asarray(np.stack([_band_bias(1, BLOCK), _band_bias(4, BLOCK),
                                    _band_bias(16, BLOCK)]))
    to_res16 = _tile_permutation()
    from_res16 = to_res16.T
    ffn_w = (ffn_w_gate.astype(BF16), ffn_w_up.astype(BF16), ffn_w_down.astype(BF16))
    x = x.reshape(BATCH * SEQ, D_MODEL)
    for layer in range(DEPTH):
        g = norm_g[layer]
        i = layer // 2
        x = _ffn_call(x, g[0:1], ffn_w, layer, 0)
        if layer % 2 == 0:
            w_in = jnp.concatenate([ev_w_in[i][:, :1024], ev_w_in[i][:, 1024 + _EV_Q_PERM],
                                    ev_w_in[i][:, 1536:]], axis=1).astype(BF16)
            u, q, k, v = _even_in_call(x, g[1:2], w_in, bd,
                                       _tile_heads(ev_q_norm_g[i], 8), _tile_heads(ev_k_norm_g[i], 2))
            sink = ev_sinks[i][_EV_HEAD_ORDER]
            r3 = lambda t: t.reshape(BATCH, SEQ, t.shape[-1])
            shift = _qk_shift(ev_q_norm_g[i], ev_k_norm_g[i])
            use_shift = jnp.logical_and(shift <= MAX_CONST_SHIFT,
                                        jnp.max(sink) - shift <= MAX_SINK_OVER_SHIFT)
            sink_terms = jnp.repeat(jnp.exp(sink - shift), HEAD_DIM).reshape(SWA_Q_HEADS // 2, 1, LANES)
            mix = lambda sk, bs, exact: _even_mix_call(r3(q), r3(k), r3(v), sk, bs, exact)
            fb = lax.cond(use_shift,
                          lambda: mix(sink_terms, ev_bias - shift, False),
                          lambda: mix(sink.reshape(SWA_Q_HEADS, 1, 1), ev_bias, True))
            w_out = ev_w_out[i].astype(BF16)
            extra = (u, jnp.pad(ev_conv_w[i], ((0, 1), (0, 0))), ev_conv_b[i][None],
                     ev_ln_g[i][None], ev_ln_b[i][None], fb.reshape(-1, 512),
                     w_out[:512], w_out[512 + _EV_Q_PERM])
            x = _ffn_call(x, g[2:3], ffn_w, layer, 1, "even", extra)
        else:
            q, k, v, u = _odd_in_call(x, g[1:2], to_res16, od_w_in[i].astype(BF16), bd,
                                      _tile_heads(od_q_norm_g[i], 8), _tile_heads(od_k_norm_g[i], 8))
            shift = _qk_shift(od_q_norm_g[i], od_k_norm_g[i])
            att = lax.cond(shift <= MAX_CONST_SHIFT,
                           lambda: _dil_attn_call(q, k, v, od_bias - shift, False),
                           lambda: _dil_attn_call(q, k, v, od_bias, True))
            pool = _pool_call(u, od_pool_w[i].astype(BF16), od_pool_scale[i][None])
            w_out = od_w_out[i].astype(BF16)
            extra = (att, pool, from_res16, w_out[:512], w_out[512:])
            x = _ffn_call(x, g[2:3], ffn_w, layer, 1, "odd", extra)
    return x.reshape(BATCH, SEQ, D_MODEL)
```

```python
import functools

import jax
import jax.numpy as jnp
import numpy as np
from jax import lax
from jax.experimental import pallas as pl
from jax.experimental.pallas import tpu as pltpu

F32 = jnp.float32
BF16 = jnp.bfloat16

D_MODEL = 1024
BATCH = 8
SEQ = 4096
DEPTH = 4
HEAD_DIM = 64
D_FF = 2752
EPS = 1e-6
NEG_INF = -1e30
BLOCK = 128
CONV_CH = 512
CONV_WIDTH = 31
SWA_Q_HEADS = 8
SWA_WINDOW = 128
EVEN_IN = 1792
DIL_HEADS = 8
POOL_CH = 512
POOL_SIZES = (2, 4, 8, 16)
POOL_GROUP = 128
ODD_IN = 2048
QK_SCALE = HEAD_DIM ** -0.5

LANES = 128
SUBLANES = 8
MXU_DIM = 256
N_RES = 16
RES_ROWS = SEQ // N_RES
TM = 512
TM_PLAIN = 512
PLAIN_PART = 256
EV_CHUNK = 1024
CONV_HALO = 32
DIL_STEP_PAIRS = 2
DIL16_SLABS_PER_TRIP = 8
DIL4_BLOCKS_PER_TRIP = 4
DIL1_BLOCKS_PER_TRIP = 16
QK_BOUND = HEAD_DIM * QK_SCALE * 1.03
MAX_CONST_SHIFT = 20.0
MAX_SINK_OVER_SHIFT = 60.0
VMEM_LIMIT = 56 * 1024 * 1024


def _params(n_axes):
    return pltpu.CompilerParams(dimension_semantics=("arbitrary",) * n_axes,
                                vmem_limit_bytes=VMEM_LIMIT)


def _const_spec(shape):
    nd = len(shape)
    return pl.BlockSpec(shape, lambda *_: (0,) * nd, pipeline_mode=pl.Buffered(1))


def _rms(x, g):
    return x * lax.rsqrt(jnp.mean(x * x, axis=-1, keepdims=True) + EPS) * g


def _sigmoid(x, one=1.0):
    return 1.0 / (one + jnp.exp(-x))


def _dot(a, b):
    return jnp.dot(a, b, preferred_element_type=F32)


def _dot_nt(a, b):
    return lax.dot_general(a, b, (((1,), (1,)), ((), ())), preferred_element_type=F32)


def _head_sumsq(x, bd_ref):
    sq = (x * x).astype(BF16)
    bd = bd_ref[...]
    outs = [_dot(sq[:, c * MXU_DIM:(c + 1) * MXU_DIM], bd) for c in range(x.shape[1] // MXU_DIM)]
    return outs[0] if len(outs) == 1 else jnp.concatenate(outs, axis=-1)


def _head_rms(x, ss, g):
    return x * lax.rsqrt(ss * (1.0 / HEAD_DIM) + EPS) * g


def _permute_rows(perm_ref, val):
    return _dot(perm_ref[...], val).astype(BF16)


def _conv_fill(ucat, halo, u_ref):
    ucat[:CONV_HALO] = halo
    ucat[CONV_HALO:CONV_HALO + TM] = u_ref[...]
    ucat[CONV_HALO + TM:] = jnp.zeros((SUBLANES, CONV_CH), F32)


def _conv_block(blk, ucat, cw_ref, cb_ref, lng_ref, lnb_ref, out_ref):
    off = CONV_HALO - (CONV_WIDTH - 1)
    base = blk * BLOCK
    wins = [ucat[base + SUBLANES * eh:base + SUBLANES * eh + BLOCK + SUBLANES, :]
            for eh in range(CONV_HALO // SUBLANES + 1)]
    acc = jnp.broadcast_to(cb_ref[...], (BLOCK, CONV_CH))
    for el in range(SUBLANES):
        part = None
        for eh, win in enumerate(wins):
            j = SUBLANES * eh + el - off
            if 0 <= j < CONV_WIDTH:
                term = cw_ref[j:j + 1, :] * win
                part = term if part is None else part + term
        acc = acc + part[el:el + BLOCK]
    mu = jnp.mean(acc, axis=-1, keepdims=True)
    d = acc - mu
    var = jnp.mean(d * d, axis=-1, keepdims=True)
    y = d * lax.rsqrt(var + EPS) * lng_ref[...] + lnb_ref[...]
    out = y * _sigmoid(y)
    out_ref[base:base + BLOCK, :] = out.astype(BF16)
    return _dependent_zero(out[:SUBLANES, :LANES])


def _dependent_zero(t):
    bits = lax.bitcast_convert_type(t, jnp.int32)
    zero = lax.shift_right_logical(lax.shift_right_logical(bits, 16), 16)
    return jnp.max(zero.astype(F32), axis=(0, 1), keepdims=True)


def _ffn_kernel(*refs, mode):
    if mode == "even":
        (x_ref, u_ref, un_ref, cw_ref, cb_ref, lng_ref, lnb_ref, fb_ref, woa_ref, wob_ref,
         g_ref, wg_ref, wu_ref, wd_ref, o_ref, ucat, fa_scr) = refs
    elif mode == "odd":
        x_ref, fa_ref, fb_ref, perm_ref, woa_ref, wob_ref, g_ref, wg_ref, wu_ref, wd_ref, o_ref = refs
    else:
        x_ref, g_ref, wg_ref, wu_ref, wd_ref, o_ref = refs
    if mode == "plain":
        for part in range(TM_PLAIN // PLAIN_PART):
            rows = slice(part * PLAIN_PART, (part + 1) * PLAIN_PART)
            xp = x_ref[rows, :]
            xn = _rms(xp, g_ref[...]).astype(BF16)
            h = _dot(xn, wg_ref[...])
            u = _dot(xn, wu_ref[...])
            a = (h * _sigmoid(h) * u).astype(BF16)
            o_ref[rows, :] = xp + 0.5 * _dot(a, wd_ref[...])
        return
    x = x_ref[...]
    conv_next = lambda blk: 0.0
    if mode == "even":
        i = pl.program_id(0)
        conv = functools.partial(_conv_block, ucat=ucat, cw_ref=cw_ref, cb_ref=cb_ref,
                                 lng_ref=lng_ref, lnb_ref=lnb_ref)

        @pl.when(i == 0)
        def _():
            _conv_fill(ucat, jnp.zeros((CONV_HALO, CONV_CH), F32), u_ref)
            for blk in range(TM // BLOCK):
                conv(blk, out_ref=fa_scr.at[0])

        seq_start = lax.rem(i + 1, SEQ // TM) == 0
        _conv_fill(ucat, jnp.where(seq_start, 0.0, u_ref[TM - CONV_HALO:, :]), un_ref)
        conv_next = functools.partial(conv, out_ref=fa_scr.at[lax.rem(i + 1, 2)])
        x = x + _dot(fa_scr[lax.rem(i, 2)], woa_ref[...]) + _dot(fb_ref[...], wob_ref[...])
    elif mode == "odd":
        feats = [_permute_rows(perm_ref, f_ref[0].reshape(TM, f_ref.shape[-1]))
                 for f_ref in (fa_ref, fb_ref)]
        x = x + _dot(feats[0], woa_ref[...]) + _dot(feats[1], wob_ref[...])
    xn = _rms(x, g_ref[...]).astype(BF16)
    h = _dot(xn, wg_ref[...])
    u = _dot(xn, wu_ref[...])
    one = 1.0 + (conv_next(0) + conv_next(1))
    a = (h * _sigmoid(h, one) * u).astype(BF16)
    y = _dot(a, wd_ref[...])
    half = 0.5 + (conv_next(2) + conv_next(3))
    o_ref[...] = x + half * y


def _row_spec(width, rows=TM):
    return pl.BlockSpec((rows, width), lambda i: (i, 0))


def _res16_spec(width):
    tiles = SEQ // TM
    return pl.BlockSpec((1, N_RES, TM // N_RES, width), lambda i: (i // tiles, 0, i % tiles, 0))


def _stack_spec(w, layer, j):
    return pl.BlockSpec((None, None) + w.shape[2:], lambda i: (layer, j, 0, 0),
                        pipeline_mode=pl.Buffered(1))


def _ffn_call(x, g, ffn_w, layer, j, mode="plain", extra=()):
    rows = TM_PLAIN if mode == "plain" else TM
    n_tiles = BATCH * SEQ // rows
    in_specs = [_row_spec(D_MODEL, rows)]
    args = [x]
    scratch = []
    if mode == "odd":
        fa, fb, perm, woa, wob = extra
        in_specs += [_res16_spec(fa.shape[-1]), _res16_spec(fb.shape[-1])]
        in_specs += [_const_spec(t.shape) for t in (perm, woa, wob)]
        args += [fa, fb, perm, woa, wob]
    elif mode == "even":
        u, cw, cb, lng, lnb, fb, woa, wob = extra
        nxt = pl.BlockSpec((TM, CONV_CH), lambda i: (jnp.minimum(i + 1, n_tiles - 1), 0))
        in_specs += [_row_spec(CONV_CH), nxt]
        in_specs += [_const_spec(t.shape) for t in (cw, cb, lng, lnb)]
        in_specs += [_row_spec(fb.shape[-1]), _const_spec(woa.shape), _const_spec(wob.shape)]
        args += [u, u, cw, cb, lng, lnb, fb, woa, wob]
        scratch = [pltpu.VMEM((CONV_HALO + TM + SUBLANES, CONV_CH), F32),
                   pltpu.VMEM((2, TM, CONV_CH), BF16)]
    in_specs += [_const_spec(g.shape)] + [_stack_spec(w, layer, j) for w in ffn_w]
    args += [g, *ffn_w]
    return pl.pallas_call(
        functools.partial(_ffn_kernel, mode=mode),
        grid=(n_tiles,), in_specs=in_specs, out_specs=_row_spec(D_MODEL, rows),
        out_shape=jax.ShapeDtypeStruct(x.shape, F32), scratch_shapes=scratch,
        compiler_params=_params(1), name="ffn_" + mode,
    )(*args)


def _even_in_kernel(x_ref, g_ref, w_ref, bd_ref, gq_ref, gk_ref, u_ref, q_ref, k_ref, v_ref):
    hn = _rms(x_ref[...], g_ref[...]).astype(BF16)
    proj = _dot(hn, w_ref[...])
    a_val = proj[:, :CONV_CH]
    a_gate = proj[:, CONV_CH:2 * CONV_CH]
    u_ref[...] = a_val * _sigmoid(a_gate)
    qkv = proj[:, 2 * CONV_CH:]
    ss = _head_sumsq(qkv, bd_ref)
    q = qkv[:, :512]
    k = qkv[:, 512:640]
    q_ref[...] = (_head_rms(q, ss[:, :512], gq_ref[...]) * QK_SCALE).astype(BF16)
    k_ref[...] = _head_rms(k, ss[:, 512:640], gk_ref[...]).astype(BF16)
    v_ref[...] = qkv[:, 640:].astype(BF16)


def _even_in_call(x, g, w_in, bd, gq, gk):
    n = x.shape[0]
    row = lambda width: pl.BlockSpec((TM, width), lambda i: (i, 0))
    return pl.pallas_call(
        _even_in_kernel, grid=(n // TM,),
        in_specs=[row(D_MODEL), _const_spec(g.shape), _const_spec(w_in.shape),
                  _const_spec(bd.shape), _const_spec(gq.shape), _const_spec(gk.shape)],
        out_specs=[row(512), row(512), row(128), row(128)],
        out_shape=[jax.ShapeDtypeStruct((n, 512), F32), jax.ShapeDtypeStruct((n, 512), BF16),
                   jax.ShapeDtypeStruct((n, 128), BF16), jax.ShapeDtypeStruct((n, 128), BF16)],
        compiler_params=_params(1), name="even_in",
    )(x, g, w_in, bd, gq, gk)


def _pv_operand(vv):
    lo = lax.broadcasted_iota(jnp.int32, vv.shape, 1) < HEAD_DIM
    zero = jnp.zeros_like(vv)
    ones_lo = jnp.where(lo, 1.0, 0.0)
    top = jnp.concatenate([jnp.where(lo, vv, zero), ones_lo.astype(BF16)], axis=-1)
    bot = jnp.concatenate([jnp.where(lo, zero, vv), (1.0 - ones_lo).astype(BF16)], axis=-1)
    return jnp.concatenate([top, bot], axis=0)


def _attend(s, bias, vv, exact, sink=None):
    nh = s.shape[0] // BLOCK
    s3 = s.reshape(nh, BLOCK, s.shape[1]) + bias[None]
    lo_q = lax.broadcasted_iota(jnp.int32, (BLOCK, LANES), 1) < HEAD_DIM
    m = None
    if exact:
        m = jnp.max(s3, axis=-1, keepdims=True)
        if sink is not None:
            m = jnp.maximum(m, sink)
        s3 = s3 - m
        if sink is not None:
            sink = jnp.exp(sink - m)
    p = jnp.exp(s3)
    vext = _pv_operand(vv)
    outs = []
    for pr in range(nh // 2):
        pcat = jnp.concatenate([p[2 * pr], p[2 * pr + 1]], axis=-1).astype(BF16)
        res = _dot(pcat, vext)
        if sink is not None:
            term = jnp.where(lo_q, sink[2 * pr], sink[2 * pr + 1]) if exact else sink[pr]
            res = jnp.concatenate([res[:, :LANES], res[:, LANES:] + term], axis=-1)
        m_pair = jnp.where(lo_q, m[2 * pr], m[2 * pr + 1]) if exact else None
        outs.append((res, m_pair))
    return outs


def _stack_pairs(q_pairs):
    lo = lax.broadcasted_iota(jnp.int32, (BLOCK, LANES), 1) < HEAD_DIM
    parts = []
    for q in q_pairs:
        zero = jnp.zeros_like(q)
        parts += [jnp.where(lo, q, zero), jnp.where(lo, zero, q)]
    return jnp.concatenate(parts, axis=0)


def _even_mix_kernel(q_ref, k_ref, kh_ref, v_ref, vh_ref, sink_ref, bias_ref, fb_ref,
                     kcat, vcat, *, exact):
    first = pl.program_id(1) == 0
    kcat[:BLOCK] = kh_ref[0]
    kcat[BLOCK:] = k_ref[0]
    vcat[:BLOCK] = vh_ref[0]
    vcat[BLOCK:] = v_ref[0]
    sink = sink_ref[...]

    def attn_block(i, carry):
        base = pl.multiple_of(i * BLOCK, BLOCK)
        kk = kcat[pl.ds(base, 2 * BLOCK), :]
        vv = vcat[pl.ds(base, 2 * BLOCK), :]
        qs = _stack_pairs([q_ref[0, pl.ds(base, BLOCK), pr * LANES:(pr + 1) * LANES]
                           for pr in range(SWA_Q_HEADS // 2)])
        bias = bias_ref[jnp.where(jnp.logical_and(first, i == 0), 1, 0)]
        s = _dot_nt(qs, kk)
        for pr, (res, _) in enumerate(_attend(s, bias, vv, exact, sink)):
            out = res[:, :LANES] / res[:, LANES:]
            fb_ref[0, pl.ds(base, BLOCK), pr * LANES:(pr + 1) * LANES] = out.astype(BF16)
        return carry

    lax.fori_loop(0, EV_CHUNK // BLOCK, attn_block, 0, unroll=4)


def _even_mix_call(q, k, v, sink, bias, exact):
    nchunk = SEQ // EV_CHUNK
    cur = lambda width: pl.BlockSpec((1, EV_CHUNK, width), lambda b, c: (b, c, 0))
    halo = pl.BlockSpec((1, BLOCK, LANES),
                        lambda b, c: (b, jnp.maximum(c * (EV_CHUNK // BLOCK) - 1, 0), 0))
    return pl.pallas_call(
        functools.partial(_even_mix_kernel, exact=exact), grid=(BATCH, nchunk),
        in_specs=[cur(512), cur(128), halo, cur(128), halo, _const_spec(sink.shape),
                  _const_spec(bias.shape)],
        out_specs=cur(512),
        out_shape=jax.ShapeDtypeStruct((BATCH, SEQ, 512), BF16),
        scratch_shapes=[pltpu.VMEM((BLOCK + EV_CHUNK, LANES), BF16),
                        pltpu.VMEM((BLOCK + EV_CHUNK, LANES), BF16)],
        compiler_params=_params(2), name="even_mix_exact" if exact else "even_mix",
    )(q, k, k, v, v, sink, bias)


def _odd_in_kernel(x_ref, g_ref, perm_ref, w_ref, bd_ref, gq_ref, gk_ref,
                   q_ref, k_ref, v_ref, u_ref):
    hn = _rms(x_ref[...], g_ref[...]).astype(BF16)
    hn = _permute_rows(perm_ref, hn)
    proj = _dot(hn, w_ref[...])
    qk = proj[:, :1024]
    ss = _head_sumsq(qk, bd_ref)
    q = _head_rms(qk[:, :512], ss[:, :512], gq_ref[...]) * QK_SCALE
    k = _head_rms(qk[:, 512:], ss[:, 512:], gk_ref[...])
    v = proj[:, 1024:1536]
    u = proj[:, 1536:]
    rows = TM // N_RES
    for ref, val in ((q_ref, q), (k_ref, k), (v_ref, v), (u_ref, u)):
        for r in range(N_RES):
            ref[0, r] = val[r * rows:(r + 1) * rows]


def _odd_in_call(x, g, perm, w_in, bd, gq, gk):
    shp = jax.ShapeDtypeStruct((BATCH, N_RES, RES_ROWS, 512), F32)
    return pl.pallas_call(
        _odd_in_kernel, grid=(BATCH * SEQ // TM,),
        in_specs=[_row_spec(D_MODEL), _const_spec(g.shape), _const_spec(perm.shape),
                  _const_spec(w_in.shape), _const_spec(bd.shape), _const_spec(gq.shape),
                  _const_spec(gk.shape)],
        out_specs=[_res16_spec(512)] * 4,
        out_shape=[shp] * 4,
        compiler_params=_params(1), name="odd_in",
    )(x, g, perm, w_in, bd, gq, gk)


def _dil_attn_kernel(q_ref, k_ref, v_ref, bias_ref, o_ref, st, *m_scratch, exact):
    m_s = m_scratch[0] if exact else None

    def lanes(pi):
        return slice(pi * LANES, (pi + 1) * LANES)

    def gather(ref, slabs, start, rows, pi):
        parts = [ref[0, s, pl.ds(start, rows), lanes(pi)] for s in slabs]
        return jnp.concatenate(parts, axis=0).astype(BF16)

    def block(q, kk, vv, bias):
        return _attend(_dot_nt(_stack_pairs([q]), kk), bias, vv, exact)[0]

    def put(pi, slab, rows, res, m, init):
        idx = (pi, slab, rows, slice(None))
        if init:
            st[idx] = res
            if exact:
                m_s[idx] = m
        elif not exact:
            st[idx] += res
        else:
            m_old = m_s[idx]
            m_new = jnp.maximum(m_old, m)
            a = jnp.exp(m_old - m_new)
            b = jnp.exp(m - m_new)
            m_s[idx] = m_new
            st[idx] = (jnp.concatenate([a, a], axis=-1) * st[idx]
                       + jnp.concatenate([b, b], axis=-1) * res)

    def put_chunks(pi, slabs, start, rows, res, m):
        for c, s in enumerate(slabs):
            part = slice(c * rows, (c + 1) * rows)
            put(pi, s, pl.ds(start, rows), res[part], m[part] if exact else None, False)

    def branch16(i, carry):
        bias = bias_ref[0, 0]
        for r in (DIL16_SLABS_PER_TRIP * i + dr for dr in range(DIL16_SLABS_PER_TRIP)):
            for pi in range(DIL_STEP_PAIRS):
                q, k, v = (ref[0, r, :, lanes(pi)].astype(BF16) for ref in (q_ref, k_ref, v_ref))
                res, m = block(q[:BLOCK], k[:BLOCK], v[:BLOCK], bias[:, BLOCK:])
                put(pi, r, slice(0, BLOCK), res, m, True)
                res, m = block(q[BLOCK:], k, v, bias)
                put(pi, r, slice(BLOCK, 2 * BLOCK), res, m, True)
        return carry

    lax.fori_loop(0, N_RES // DIL16_SLABS_PER_TRIP, branch16, 0)

    def strided_block(bi, nslab, r0, n, first_possible):
        rows = BLOCK // nslab
        cur = pl.multiple_of(n * rows, rows)
        prev = pl.multiple_of(jnp.maximum(n - 1, 0) * rows, rows)
        slabs = [r0 + (N_RES // nslab) * c for c in range(nslab)]
        bias = bias_ref[bi, jnp.where(n == 0, 1, 0)] if first_possible else bias_ref[bi, 0]
        for pi in range(DIL_STEP_PAIRS):
            kk = jnp.concatenate([gather(k_ref, slabs, prev, rows, pi),
                                  gather(k_ref, slabs, cur, rows, pi)], axis=0)
            vv = jnp.concatenate([gather(v_ref, slabs, prev, rows, pi),
                                  gather(v_ref, slabs, cur, rows, pi)], axis=0)
            res, m = block(gather(q_ref, slabs, cur, rows, pi), kk, vv, bias)
            put_chunks(pi, slabs, cur, rows, res, m)

    def branch4(i, carry):
        for dn in range(DIL4_BLOCKS_PER_TRIP):
            for r0 in range(4):
                strided_block(1, 4, r0, DIL4_BLOCKS_PER_TRIP * i + dn, dn == 0)
        return carry

    lax.fori_loop(0, RES_ROWS // (BLOCK // 4) // DIL4_BLOCKS_PER_TRIP, branch4, 0)

    def branch1(i, carry):
        for dn in range(DIL1_BLOCKS_PER_TRIP):
            strided_block(2, N_RES, 0, DIL1_BLOCKS_PER_TRIP * i + dn, dn == 0)
        return carry

    lax.fori_loop(0, RES_ROWS // (BLOCK // N_RES) // DIL1_BLOCKS_PER_TRIP, branch1, 0)

    for pi in range(DIL_STEP_PAIRS):
        o_ref[0, :, :, lanes(pi)] = (st[pi, :, :, :LANES] / st[pi, :, :, LANES:]).astype(BF16)


def _dil_attn_call(q, k, v, bias, exact):
    width = DIL_STEP_PAIRS * LANES
    blk = lambda: pl.BlockSpec((1, N_RES, RES_ROWS, width), lambda b, p: (b, 0, 0, p))
    scratch = [pltpu.VMEM((DIL_STEP_PAIRS, N_RES, RES_ROWS, 2 * LANES), F32)]
    if exact:
        scratch.append(pltpu.VMEM((DIL_STEP_PAIRS, N_RES, RES_ROWS, LANES), F32))
    return pl.pallas_call(
        functools.partial(_dil_attn_kernel, exact=exact),
        grid=(BATCH, DIL_HEADS // 2 // DIL_STEP_PAIRS),
        in_specs=[blk(), blk(), blk(), _const_spec(bias.shape)],
        out_specs=blk(),
        out_shape=jax.ShapeDtypeStruct((BATCH, N_RES, RES_ROWS, 512), BF16),
        scratch_shapes=scratch,
        compiler_params=_params(2), name="dil_attn_exact" if exact else "dil_attn",
    )(q, k, v, bias)


def _pool_kernel(u_ref, w_ref, sc_ref, o_ref):
    row0 = lax.broadcasted_iota(jnp.int32, (RES_ROWS, POOL_GROUP), 0) == 0

    def shift(x):
        return jnp.where(row0, 0.0, pltpu.roll(x, 1, axis=0))

    for gi, w in enumerate(POOL_SIZES):
        lanes = slice(gi * POOL_GROUP, (gi + 1) * POOL_GROUP)
        slab = [u_ref[0, r, :, lanes] for r in range(N_RES)]
        pre = [slab[0]]
        for r in range(1, N_RES):
            pre.append(pre[-1] + slab[r])
        tot_sh = shift(pre[-1])
        pooled = []
        for r in range(N_RES):
            if r >= w:
                win = pre[r] - pre[r - w]
            elif r == w - 1:
                win = pre[r]
            else:
                win = pre[r] + (tot_sh - shift(pre[r - w + N_RES]))
            inv = jnp.where(row0, 1.0 / min(r + 1, w), 1.0 / w)
            pooled.append((win * inv - slab[r]).astype(BF16))
        out = _dot(jnp.concatenate(pooled, axis=0), w_ref[gi]) * sc_ref[:, lanes]
        for r in range(N_RES):
            o_ref[0, r, :, lanes] = out[r * RES_ROWS:(r + 1) * RES_ROWS].astype(BF16)


def _pool_call(u, w, sc):
    blk = lambda: pl.BlockSpec((1, N_RES, RES_ROWS, POOL_CH), lambda b: (b, 0, 0, 0))
    return pl.pallas_call(
        _pool_kernel, grid=(BATCH,),
        in_specs=[blk(), _const_spec(w.shape), _const_spec(sc.shape)],
        out_specs=blk(),
        out_shape=jax.ShapeDtypeStruct((BATCH, N_RES, RES_ROWS, POOL_CH), BF16),
        compiler_params=_params(1), name="pool",
    )(u, w, sc)


def _band_bias(nslab, max_dist):
    rows = BLOCK // nslab
    s = np.arange(BLOCK)
    pos = nslab * (s % rows) + s // rows
    dist = pos[:, None] - np.concatenate([pos - BLOCK, pos])[None, :]
    gen = np.where((dist >= 0) & (dist <= max_dist), 0.0, NEG_INF).astype(np.float32)
    fst = gen.copy()
    fst[:, :BLOCK] = NEG_INF
    return np.stack([gen, fst])


def _block_diag_ones():
    i = np.arange(MXU_DIM) // HEAD_DIM
    return jnp.asarray((i[:, None] == i[None, :]).astype(np.float32), dtype=BF16)


def _tile_permutation():
    rows = TM // N_RES
    dst = np.arange(TM)
    src = (dst % rows) * N_RES + dst // rows
    return jnp.asarray((src[:, None] == np.arange(TM)[None, :]).astype(np.float32), dtype=BF16)


def _tile_heads(g, n):
    return jnp.tile(g, n)[None, :]


def _qk_shift(gq, gk):
    return QK_BOUND * jnp.max(jnp.abs(gq)) * jnp.max(jnp.abs(gk))


_EV_HEAD_ORDER = np.array([h for p in range(4) for h in (p, 4 + p)])
_EV_Q_PERM = (_EV_HEAD_ORDER[:, None] * HEAD_DIM + np.arange(HEAD_DIM)[None, :]).reshape(-1)


def kernel(x, norm_g, ffn_w_gate, ffn_w_up, ffn_w_down, ev_w_in, ev_w_out, ev_conv_w, ev_conv_b,
           ev_ln_g, ev_ln_b, ev_q_norm_g, ev_k_norm_g, ev_sinks, od_w_in, od_w_out, od_q_norm_g,
           od_k_norm_g, od_pool_w, od_pool_scale):
    bd = _block_diag_ones()
    ev_bias = jnp.asarray(_band_bias(1, SWA_WINDOW - 1))
    od_bias = jnp.asarray(np.stack([_band_bias(1, BLOCK), _band_bias(4, BLOCK),
                                    _band_bias(16, BLOCK)]))
    to_res16 = _tile_permutation()
    from_res16 = to_res16.T
    ffn_w = (ffn_w_gate.astype(BF16), ffn_w_up.astype(BF16), ffn_w_down.astype(BF16))
    x = x.reshape(BATCH * SEQ, D_MODEL)
    for layer in range(DEPTH):
        g = norm_g[layer]
        i = layer // 2
        x = _ffn_call(x, g[0:1], ffn_w, layer, 0)
        if layer % 2 == 0:
            w_in = jnp.concatenate([ev_w_in[i][:, :1024], ev_w_in[i][:, 1024 + _EV_Q_PERM],
                                    ev_w_in[i][:, 1536:]], axis=1).astype(BF16)
            u, q, k, v = _even_in_call(x, g[1:2], w_in, bd,
                                       _tile_heads(ev_q_norm_g[i], 8), _tile_heads(ev_k_norm_g[i], 2))
            sink = ev_sinks[i][_EV_HEAD_ORDER]
            r3 = lambda t: t.reshape(BATCH, SEQ, t.shape[-1])
            shift = _qk_shift(ev_q_norm_g[i], ev_k_norm_g[i])
            use_shift = jnp.logical_and(shift <= MAX_CONST_SHIFT,
                                        jnp.max(sink) - shift <= MAX_SINK_OVER_SHIFT)
            sink_terms = jnp.repeat(jnp.exp(sink - shift), HEAD_DIM).reshape(SWA_Q_HEADS // 2, 1, LANES)
            mix = lambda sk, bs, exact: _even_mix_call(r3(q), r3(k), r3(v), sk, bs, exact)
            fb = lax.cond(use_shift,
                          lambda: mix(sink_terms, ev_bias - shift, False),
                          lambda: mix(sink.reshape(SWA_Q_HEADS, 1, 1), ev_bias, True))
            w_out = ev_w_out[i].astype(BF16)
            extra = (u, jnp.pad(ev_conv_w[i], ((0, 1), (0, 0))), ev_conv_b[i][None],
                     ev_ln_g[i][None], ev_ln_b[i][None], fb.reshape(-1, 512),
                     w_out[:512], w_out[512 + _EV_Q_PERM])
            x = _ffn_call(x, g[2:3], ffn_w, layer, 1, "even", extra)
        else:
            q, k, v, u = _odd_in_call(x, g[1:2], to_res16, od_w_in[i].astype(BF16), bd,
                                      _tile_heads(od_q_norm_g[i], 8), _tile_heads(od_k_norm_g[i], 8))
            shift = _qk_shift(od_q_norm_g[i], od_k_norm_g[i])
            att = lax.cond(shift <= MAX_CONST_SHIFT,
                           lambda: _dil_attn_call(q, k, v, od_bias - shift, False),
                           lambda: _dil_attn_call(q, k, v, od_bias, True))
            pool = _pool_call(u, od_pool_w[i].astype(BF16), od_pool_scale[i][None])
            w_out = od_w_out[i].astype(BF16)
            extra = (att, pool, from_res16, w_out[:512], w_out[512:])
            x = _ffn_call(x, g[2:3], ffn_w, layer, 1, "odd", extra)
    return x.reshape(BATCH, SEQ, D_MODEL)
```

```python
import functools

import jax
import jax.numpy as jnp
import numpy as np
from jax import lax
from jax.experimental import pallas as pl
from jax.experimental.pallas import tpu as pltpu

F32 = jnp.float32
BF16 = jnp.bfloat16

D_MODEL = 1024
BATCH = 8
SEQ = 4096
DEPTH = 4
HEAD_DIM = 64
D_FF = 2752
EPS = 1e-6
NEG_INF = -1e30
BLOCK = 128
CONV_CH = 512
CONV_WIDTH = 31
SWA_Q_HEADS = 8
SWA_WINDOW = 128
EVEN_IN = 1792
DIL_HEADS = 8
POOL_CH = 512
POOL_SIZES = (2, 4, 8, 16)
POOL_GROUP = 128
ODD_IN = 2048
QK_SCALE = HEAD_DIM ** -0.5

LANES = 128
SUBLANES = 8
MXU_DIM = 256
N_RES = 16
RES_ROWS = SEQ // N_RES
TM = 512
TM_PLAIN = 512
PLAIN_PART = 256
EV_CHUNK = 1024
CONV_HALO = 32
DIL_STEP_PAIRS = 2
DIL16_SLABS_PER_TRIP = 8
DIL4_BLOCKS_PER_TRIP = 4
DIL1_BLOCKS_PER_TRIP = 16
QK_BOUND = HEAD_DIM * QK_SCALE * 1.03
MAX_CONST_SHIFT = 20.0
MAX_SINK_OVER_SHIFT = 60.0
VMEM_LIMIT = 56 * 1024 * 1024


def _params(n_axes):
    return pltpu.CompilerParams(dimension_semantics=("arbitrary",) * n_axes,
                                vmem_limit_bytes=VMEM_LIMIT)


def _const_spec(shape):
    nd = len(shape)
    return pl.BlockSpec(shape, lambda *_: (0,) * nd, pipeline_mode=pl.Buffered(1))


def _rms(x, g):
    return x * lax.rsqrt(jnp.mean(x * x, axis=-1, keepdims=True) + EPS) * g


def _sigmoid(x, one=1.0):
    return 1.0 / (one + jnp.exp(-x))


def _dot(a, b):
    return jnp.dot(a, b, preferred_element_type=F32)


def _dot_nt(a, b):
    return lax.dot_general(a, b, (((1,), (1,)), ((), ())), preferred_element_type=F32)


def _head_sumsq(x, bd_ref):
    sq = (x * x).astype(BF16)
    bd = bd_ref[...]
    outs = [_dot(sq[:, c * MXU_DIM:(c + 1) * MXU_DIM], bd) for c in range(x.shape[1] // MXU_DIM)]
    return outs[0] if len(outs) == 1 else jnp.concatenate(outs, axis=-1)


def _head_rms(x, ss, g):
    return x * lax.rsqrt(ss * (1.0 / HEAD_DIM) + EPS) * g


def _permute_rows(perm_ref, val):
    return _dot(perm_ref[...], val).astype(BF16)


def _conv_fill(ucat, halo, u_ref):
    ucat[:CONV_HALO] = halo
    ucat[CONV_HALO:CONV_HALO + TM] = u_ref[...]
    ucat[CONV_HALO + TM:] = jnp.zeros((SUBLANES, CONV_CH), F32)


def _conv_block(blk, ucat, cw_ref, cb_ref, lng_ref, lnb_ref, out_ref):
    off = CONV_HALO - (CONV_WIDTH - 1)
    base = blk * BLOCK
    accs = []
    for c in range(CONV_CH // LANES):
        lanes = slice(c * LANES, (c + 1) * LANES)
        wins = [ucat[base + SUBLANES * eh:base + SUBLANES * eh + BLOCK + SUBLANES, lanes]
                for eh in range(CONV_HALO // SUBLANES + 1)]
        acc = jnp.broadcast_to(cb_ref[:, lanes], (BLOCK, LANES))
        for el in range(SUBLANES):
            part = None
            for eh, win in enumerate(wins):
                j = SUBLANES * eh + el - off
                if 0 <= j < CONV_WIDTH:
                    term = cw_ref[j:j + 1, lanes] * win
                    part = term if part is None else part + term
            acc = acc + part[el:el + BLOCK]
        accs.append(acc)
    acc = jnp.concatenate(accs, axis=-1)
    mu = jnp.mean(acc, axis=-1, keepdims=True)
    d = acc - mu
    var = jnp.mean(d * d, axis=-1, keepdims=True)
    y = d * lax.rsqrt(var + EPS) * lng_ref[...] + lnb_ref[...]
    out = y * _sigmoid(y)
    out_ref[base:base + BLOCK, :] = out.astype(BF16)
    return _dependent_zero(out[:SUBLANES, :LANES])


def _dependent_zero(t):
    bits = lax.bitcast_convert_type(t, jnp.int32)
    zero = lax.shift_right_logical(lax.shift_right_logical(bits, 16), 16)
    return jnp.max(zero.astype(F32), axis=(0, 1), keepdims=True)


def _ffn_kernel(*refs, mode):
    if mode == "even":
        (x_ref, u_ref, un_ref, cw_ref, cb_ref, lng_ref, lnb_ref, fb_ref, woa_ref, wob_ref,
         g_ref, wg_ref, wu_ref, wd_ref, o_ref, ucat, fa_scr) = refs
    elif mode == "odd":
        x_ref, fa_ref, fb_ref, perm_ref, woa_ref, wob_ref, g_ref, wg_ref, wu_ref, wd_ref, o_ref = refs
    else:
        x_ref, g_ref, wg_ref, wu_ref, wd_ref, o_ref = refs
    if mode == "plain":
        for part in range(TM_PLAIN // PLAIN_PART):
            rows = slice(part * PLAIN_PART, (part + 1) * PLAIN_PART)
            xp = x_ref[rows, :]
            xn = _rms(xp, g_ref[...]).astype(BF16)
            h = _dot(xn, wg_ref[...])
            u = _dot(xn, wu_ref[...])
            a = (h * _sigmoid(h) * u).astype(BF16)
            o_ref[rows, :] = xp + 0.5 * _dot(a, wd_ref[...])
        return
    x = x_ref[...]
    conv_next = lambda blk: 0.0
    if mode == "even":
        i = pl.program_id(0)
        conv = functools.partial(_conv_block, ucat=ucat, cw_ref=cw_ref, cb_ref=cb_ref,
                                 lng_ref=lng_ref, lnb_ref=lnb_ref)

        @pl.when(i == 0)
        def _():
            _conv_fill(ucat, jnp.zeros((CONV_HALO, CONV_CH), F32), u_ref)
            for blk in range(TM // BLOCK):
                conv(blk, out_ref=fa_scr.at[0])

        seq_start = lax.rem(i + 1, SEQ // TM) == 0
        _conv_fill(ucat, jnp.where(seq_start, 0.0, u_ref[TM - CONV_HALO:, :]), un_ref)
        conv_next = functools.partial(conv, out_ref=fa_scr.at[lax.rem(i + 1, 2)])
        x = x + _dot(fa_scr[lax.rem(i, 2)], woa_ref[...]) + _dot(fb_ref[...], wob_ref[...])
    elif mode == "odd":
        feats = [_permute_rows(perm_ref, f_ref[0].reshape(TM, f_ref.shape[-1]))
                 for f_ref in (fa_ref, fb_ref)]
        x = x + _dot(feats[0], woa_ref[...]) + _dot(feats[1], wob_ref[...])
    xn = _rms(x, g_ref[...]).astype(BF16)
    h = _dot(xn, wg_ref[...])
    u = _dot(xn, wu_ref[...])
    one = 1.0 + (conv_next(0) + conv_next(1))
    a = (h * _sigmoid(h, one) * u).astype(BF16)
    y = _dot(a, wd_ref[...])
    half = 0.5 + (conv_next(2) + conv_next(3))
    o_ref[...] = x + half * y


def _row_spec(width, rows=TM):
    return pl.BlockSpec((rows, width), lambda i: (i, 0))


def _res16_spec(width):
    tiles = SEQ // TM
    return pl.BlockSpec((1, N_RES, TM // N_RES, width), lambda i: (i // tiles, 0, i % tiles, 0))


def _stack_spec(w, layer, j):
    return pl.BlockSpec((None, None) + w.shape[2:], lambda i: (layer, j, 0, 0),
                        pipeline_mode=pl.Buffered(1))


def _ffn_call(x, g, ffn_w, layer, j, mode="plain", extra=()):
    rows = TM_PLAIN if mode == "plain" else TM
    n_tiles = BATCH * SEQ // rows
    in_specs = [_row_spec(D_MODEL, rows)]
    args = [x]
    scratch = []
    if mode == "odd":
        fa, fb, perm, woa, wob = extra
        in_specs += [_res16_spec(fa.shape[-1]), _res16_spec(fb.shape[-1])]
        in_specs += [_const_spec(t.shape) for t in (perm, woa, wob)]
        args += [fa, fb, perm, woa, wob]
    elif mode == "even":
        u, cw, cb, lng, lnb, fb, woa, wob = extra
        nxt = pl.BlockSpec((TM, CONV_CH), lambda i: (jnp.minimum(i + 1, n_tiles - 1), 0))
        in_specs += [_row_spec(CONV_CH), nxt]
        in_specs += [_const_spec(t.shape) for t in (cw, cb, lng, lnb)]
        in_specs += [_row_spec(fb.shape[-1]), _const_spec(woa.shape), _const_spec(wob.shape)]
        args += [u, u, cw, cb, lng, lnb, fb, woa, wob]
        scratch = [pltpu.VMEM((CONV_HALO + TM + SUBLANES, CONV_CH), F32),
                   pltpu.VMEM((2, TM, CONV_CH), BF16)]
    in_specs += [_const_spec(g.shape)] + [_stack_spec(w, layer, j) for w in ffn_w]
    args += [g, *ffn_w]
    return pl.pallas_call(
        functools.partial(_ffn_kernel, mode=mode),
        grid=(n_tiles,), in_specs=in_specs, out_specs=_row_spec(D_MODEL, rows),
        out_shape=jax.ShapeDtypeStruct(x.shape, F32), scratch_shapes=scratch,
        compiler_params=_params(1), name="ffn_" + mode,
    )(*args)


def _even_in_kernel(x_ref, g_ref, w_ref, bd_ref, gq_ref, gk_ref, u_ref, q_ref, k_ref, v_ref):
    hn = _rms(x_ref[...], g_ref[...]).astype(BF16)
    proj = _dot(hn, w_ref[...])
    a_val = proj[:, :CONV_CH]
    a_gate = proj[:, CONV_CH:2 * CONV_CH]
    u_ref[...] = a_val * _sigmoid(a_gate)
    qkv = proj[:, 2 * CONV_CH:]
    ss = _head_sumsq(qkv, bd_ref)
    q = qkv[:, :512]
    k = qkv[:, 512:640]
    q_ref[...] = (_head_rms(q, ss[:, :512], gq_ref[...]) * QK_SCALE).astype(BF16)
    k_ref[...] = _head_rms(k, ss[:, 512:640], gk_ref[...]).astype(BF16)
    v_ref[...] = qkv[:, 640:].astype(BF16)


def _even_in_call(x, g, w_in, bd, gq, gk):
    n = x.shape[0]
    row = lambda width: pl.BlockSpec((TM, width), lambda i: (i, 0))
    return pl.pallas_call(
        _even_in_kernel, grid=(n // TM,),
        in_specs=[row(D_MODEL), _const_spec(g.shape), _const_spec(w_in.shape),
                  _const_spec(bd.shape), _const_spec(gq.shape), _const_spec(gk.shape)],
        out_specs=[row(512), row(512), row(128), row(128)],
        out_shape=[jax.ShapeDtypeStruct((n, 512), F32), jax.ShapeDtypeStruct((n, 512), BF16),
                   jax.ShapeDtypeStruct((n, 128), BF16), jax.ShapeDtypeStruct((n, 128), BF16)],
        compiler_params=_params(1), name="even_in",
    )(x, g, w_in, bd, gq, gk)


def _pv_operand(vv):
    lo = lax.broadcasted_iota(jnp.int32, vv.shape, 1) < HEAD_DIM
    zero = jnp.zeros_like(vv)
    ones_lo = jnp.where(lo, 1.0, 0.0)
    top = jnp.concatenate([jnp.where(lo, vv, zero), ones_lo.astype(BF16)], axis=-1)
    bot = jnp.concatenate([jnp.where(lo, zero, vv), (1.0 - ones_lo).astype(BF16)], axis=-1)
    return jnp.concatenate([top, bot], axis=0)


def _attend(s, bias, vv, exact, sink=None):
    nh = s.shape[0] // BLOCK
    s3 = s.reshape(nh, BLOCK, s.shape[1]) + bias[None]
    lo_q = lax.broadcasted_iota(jnp.int32, (BLOCK, LANES), 1) < HEAD_DIM
    m = None
    if exact:
        m = jnp.max(s3, axis=-1, keepdims=True)
        if sink is not None:
            m = jnp.maximum(m, sink)
        s3 = s3 - m
        if sink is not None:
            sink = jnp.exp(sink - m)
    p = jnp.exp(s3)
    vext = _pv_operand(vv)
    outs = []
    for pr in range(nh // 2):
        pcat = jnp.concatenate([p[2 * pr], p[2 * pr + 1]], axis=-1).astype(BF16)
        res = _dot(pcat, vext)
        if sink is not None:
            term = jnp.where(lo_q, sink[2 * pr], sink[2 * pr + 1]) if exact else sink[pr]
            res = jnp.concatenate([res[:, :LANES], res[:, LANES:] + term], axis=-1)
        m_pair = jnp.where(lo_q, m[2 * pr], m[2 * pr + 1]) if exact else None
        outs.append((res, m_pair))
    return outs


def _stack_pairs(q_pairs):
    lo = lax.broadcasted_iota(jnp.int32, (BLOCK, LANES), 1) < HEAD_DIM
    parts = []
    for q in q_pairs:
        zero = jnp.zeros_like(q)
        parts += [jnp.where(lo, q, zero), jnp.where(lo, zero, q)]
    return jnp.concatenate(parts, axis=0)


def _even_mix_kernel(q_ref, k_ref, kh_ref, v_ref, vh_ref, sink_ref, bias_ref, fb_ref,
                     kcat, vcat, *, exact):
    first = pl.program_id(1) == 0
    kcat[:BLOCK] = kh_ref[0]
    kcat[BLOCK:] = k_ref[0]
    vcat[:BLOCK] = vh_ref[0]
    vcat[BLOCK:] = v_ref[0]
    sink = sink_ref[...]

    def attn_block(i, carry):
        base = pl.multiple_of(i * BLOCK, BLOCK)
        kk = kcat[pl.ds(base, 2 * BLOCK), :]
        vv = vcat[pl.ds(base, 2 * BLOCK), :]
        qs = _stack_pairs([q_ref[0, pl.ds(base, BLOCK), pr * LANES:(pr + 1) * LANES]
                           for pr in range(SWA_Q_HEADS // 2)])
        bias = bias_ref[jnp.where(jnp.logical_and(first, i == 0), 1, 0)]
        s = _dot_nt(qs, kk)
        for pr, (res, _) in enumerate(_attend(s, bias, vv, exact, sink)):
            out = res[:, :LANES] / res[:, LANES:]
            fb_ref[0, pl.ds(base, BLOCK), pr * LANES:(pr + 1) * LANES] = out.astype(BF16)
        return carry

    lax.fori_loop(0, EV_CHUNK // BLOCK, attn_block, 0, unroll=4)


def _even_mix_call(q, k, v, sink, bias, exact):
    nchunk = SEQ // EV_CHUNK
    cur = lambda width: pl.BlockSpec((1, EV_CHUNK, width), lambda b, c: (b, c, 0))
    halo = pl.BlockSpec((1, BLOCK, LANES),
                        lambda b, c: (b, jnp.maximum(c * (EV_CHUNK // BLOCK) - 1, 0), 0))
    return pl.pallas_call(
        functools.partial(_even_mix_kernel, exact=exact), grid=(BATCH, nchunk),
        in_specs=[cur(512), cur(128), halo, cur(128), halo, _const_spec(sink.shape),
                  _const_spec(bias.shape)],
        out_specs=cur(512),
        out_shape=jax.ShapeDtypeStruct((BATCH, SEQ, 512), BF16),
        scratch_shapes=[pltpu.VMEM((BLOCK + EV_CHUNK, LANES), BF16),
                        pltpu.VMEM((BLOCK + EV_CHUNK, LANES), BF16)],
        compiler_params=_params(2), name="even_mix_exact" if exact else "even_mix",
    )(q, k, k, v, v, sink, bias)


def _odd_in_kernel(x_ref, g_ref, perm_ref, w_ref, bd_ref, gq_ref, gk_ref,
                   q_ref, k_ref, v_ref, u_ref):
    hn = _rms(x_ref[...], g_ref[...]).astype(BF16)
    hn = _permute_rows(perm_ref, hn)
    proj = _dot(hn, w_ref[...])
    qk = proj[:, :1024]
    ss = _head_sumsq(qk, bd_ref)
    q = _head_rms(qk[:, :512], ss[:, :512], gq_ref[...]) * QK_SCALE
    k = _head_rms(qk[:, 512:], ss[:, 512:], gk_ref[...])
    v = proj[:, 1024:1536]
    u = proj[:, 1536:]
    rows = TM // N_RES
    for ref, val in ((q_ref, q), (k_ref, k), (v_ref, v), (u_ref, u)):
        for r in range(N_RES):
            ref[0, r] = val[r * rows:(r + 1) * rows]


def _odd_in_call(x, g, perm, w_in, bd, gq, gk):
    shp = jax.ShapeDtypeStruct((BATCH, N_RES, RES_ROWS, 512), F32)
    return pl.pallas_call(
        _odd_in_kernel, grid=(BATCH * SEQ // TM,),
        in_specs=[_row_spec(D_MODEL), _const_spec(g.shape), _const_spec(perm.shape),
                  _const_spec(w_in.shape), _const_spec(bd.shape), _const_spec(gq.shape),
                  _const_spec(gk.shape)],
        out_specs=[_res16_spec(512)] * 4,
        out_shape=[shp] * 4,
        compiler_params=_params(1), name="odd_in",
    )(x, g, perm, w_in, bd, gq, gk)


def _dil_attn_kernel(q_ref, k_ref, v_ref, bias_ref, o_ref, st, *m_scratch, exact):
    m_s = m_scratch[0] if exact else None

    def lanes(pi):
        return slice(pi * LANES, (pi + 1) * LANES)

    def gather(ref, slabs, start, rows, pi):
        parts = [ref[0, s, pl.ds(start, rows), lanes(pi)] for s in slabs]
        return jnp.concatenate(parts, axis=0).astype(BF16)

    def block(q, kk, vv, bias):
        return _attend(_dot_nt(_stack_pairs([q]), kk), bias, vv, exact)[0]

    def put(pi, slab, rows, res, m, init):
        idx = (pi, slab, rows, slice(None))
        if init:
            st[idx] = res
            if exact:
                m_s[idx] = m
        elif not exact:
            st[idx] += res
        else:
            m_old = m_s[idx]
            m_new = jnp.maximum(m_old, m)
            a = jnp.exp(m_old - m_new)
            b = jnp.exp(m - m_new)
            m_s[idx] = m_new
            st[idx] = (jnp.concatenate([a, a], axis=-1) * st[idx]
                       + jnp.concatenate([b, b], axis=-1) * res)

    def put_chunks(pi, slabs, start, rows, res, m):
        for c, s in enumerate(slabs):
            part = slice(c * rows, (c + 1) * rows)
            put(pi, s, pl.ds(start, rows), res[part], m[part] if exact else None, False)

    def branch16(i, carry):
        bias = bias_ref[0, 0]
        for r in (DIL16_SLABS_PER_TRIP * i + dr for dr in range(DIL16_SLABS_PER_TRIP)):
            for pi in range(DIL_STEP_PAIRS):
                q, k, v = (ref[0, r, :, lanes(pi)].astype(BF16) for ref in (q_ref, k_ref, v_ref))
                res, m = block(q[:BLOCK], k[:BLOCK], v[:BLOCK], bias[:, BLOCK:])
                put(pi, r, slice(0, BLOCK), res, m, True)
                res, m = block(q[BLOCK:], k, v, bias)
                put(pi, r, slice(BLOCK, 2 * BLOCK), res, m, True)
        return carry

    lax.fori_loop(0, N_RES // DIL16_SLABS_PER_TRIP, branch16, 0)

    def strided_block(bi, nslab, r0, n, first_possible):
        rows = BLOCK // nslab
        cur = pl.multiple_of(n * rows, rows)
        prev = pl.multiple_of(jnp.maximum(n - 1, 0) * rows, rows)
        slabs = [r0 + (N_RES // nslab) * c for c in range(nslab)]
        bias = bias_ref[bi, jnp.where(n == 0, 1, 0)] if first_possible else bias_ref[bi, 0]
        for pi in range(DIL_STEP_PAIRS):
            kk = jnp.concatenate([gather(k_ref, slabs, prev, rows, pi),
                                  gather(k_ref, slabs, cur, rows, pi)], axis=0)
            vv = jnp.concatenate([gather(v_ref, slabs, prev, rows, pi),
                                  gather(v_ref, slabs, cur, rows, pi)], axis=0)
            res, m = block(gather(q_ref, slabs, cur, rows, pi), kk, vv, bias)
            put_chunks(pi, slabs, cur, rows, res, m)

    def branch4(i, carry):
        for dn in range(DIL4_BLOCKS_PER_TRIP):
            for r0 in range(4):
                strided_block(1, 4, r0, DIL4_BLOCKS_PER_TRIP * i + dn, dn == 0)
        return carry

    lax.fori_loop(0, RES_ROWS // (BLOCK // 4) // DIL4_BLOCKS_PER_TRIP, branch4, 0)

    def branch1(i, carry):
        for dn in range(DIL1_BLOCKS_PER_TRIP):
            strided_block(2, N_RES, 0, DIL1_BLOCKS_PER_TRIP * i + dn, dn == 0)
        return carry

    lax.fori_loop(0, RES_ROWS // (BLOCK // N_RES) // DIL1_BLOCKS_PER_TRIP, branch1, 0)

    for pi in range(DIL_STEP_PAIRS):
        o_ref[0, :, :, lanes(pi)] = (st[pi, :, :, :LANES] / st[pi, :, :, LANES:]).astype(BF16)


def _dil_attn_call(q, k, v, bias, exact):
    width = DIL_STEP_PAIRS * LANES
    blk = lambda: pl.BlockSpec((1, N_RES, RES_ROWS, width), lambda b, p: (b, 0, 0, p))
    scratch = [pltpu.VMEM((DIL_STEP_PAIRS, N_RES, RES_ROWS, 2 * LANES), F32)]
    if exact:
        scratch.append(pltpu.VMEM((DIL_STEP_PAIRS, N_RES, RES_ROWS, LANES), F32))
    return pl.pallas_call(
        functools.partial(_dil_attn_kernel, exact=exact),
        grid=(BATCH, DIL_HEADS // 2 // DIL_STEP_PAIRS),
        in_specs=[blk(), blk(), blk(), _const_spec(bias.shape)],
        out_specs=blk(),
        out_shape=jax.ShapeDtypeStruct((BATCH, N_RES, RES_ROWS, 512), BF16),
        scratch_shapes=scratch,
        compiler_params=_params(2), name="dil_attn_exact" if exact else "dil_attn",
    )(q, k, v, bias)


def _pool_kernel(u_ref, w_ref, sc_ref, o_ref):
    row0 = lax.broadcasted_iota(jnp.int32, (RES_ROWS, POOL_GROUP), 0) == 0

    def shift(x):
        return jnp.where(row0, 0.0, pltpu.roll(x, 1, axis=0))

    for gi, w in enumerate(POOL_SIZES):
        lanes = slice(gi * POOL_GROUP, (gi + 1) * POOL_GROUP)
        slab = [u_ref[0, r, :, lanes] for r in range(N_RES)]
        pre = [slab[0]]
        for r in range(1, N_RES):
            pre.append(pre[-1] + slab[r])
        tot_sh = shift(pre[-1])
        pooled = []
        for r in range(N_RES):
            if r >= w:
                win = pre[r] - pre[r - w]
            elif r == w - 1:
                win = pre[r]
            else:
                win = pre[r] + (tot_sh - shift(pre[r - w + N_RES]))
            inv = jnp.where(row0, 1.0 / min(r + 1, w), 1.0 / w)
            pooled.append((win * inv - slab[r]).astype(BF16))
        out = _dot(jnp.concatenate(pooled, axis=0), w_ref[gi]) * sc_ref[:, lanes]
        for r in range(N_RES):
            o_ref[0, r, :, lanes] = out[r * RES_ROWS:(r + 1) * RES_ROWS].astype(BF16)


def _pool_call(u, w, sc):
    blk = lambda: pl.BlockSpec((1, N_RES, RES_ROWS, POOL_CH), lambda b: (b, 0, 0, 0))
    return pl.pallas_call(
        _pool_kernel, grid=(BATCH,),
        in_specs=[blk(), _const_spec(w.shape), _const_spec(sc.shape)],
        out_specs=blk(),
        out_shape=jax.ShapeDtypeStruct((BATCH, N_RES, RES_ROWS, POOL_CH), BF16),
        compiler_params=_params(1), name="pool",
    )(u, w, sc)


def _band_bias(nslab, max_dist):
    rows = BLOCK // nslab
    s = np.arange(BLOCK)
    pos = nslab * (s % rows) + s // rows
    dist = pos[:, None] - np.concatenate([pos - BLOCK, pos])[None, :]
    gen = np.where((dist >= 0) & (dist <= max_dist), 0.0, NEG_INF).astype(np.float32)
    fst = gen.copy()
    fst[:, :BLOCK] = NEG_INF
    return np.stack([gen, fst])


def _block_diag_ones():
    i = np.arange(MXU_DIM) // HEAD_DIM
    return jnp.asarray((i[:, None] == i[None, :]).astype(np.float32), dtype=BF16)


def _tile_permutation():
    rows = TM // N_RES
    dst = np.arange(TM)
    src = (dst % rows) * N_RES + dst // rows
    return jnp.asarray((src[:, None] == np.arange(TM)[None, :]).astype(np.float32), dtype=BF16)


def _tile_heads(g, n):
    return jnp.tile(g, n)[None, :]


def _qk_shift(gq, gk):
    return QK_BOUND * jnp.max(jnp.abs(gq)) * jnp.max(jnp.abs(gk))


_EV_HEAD_ORDER = np.array([h for p in range(4) for h in (p, 4 + p)])
_EV_Q_PERM = (_EV_HEAD_ORDER[:, None] * HEAD_DIM + np.arange(HEAD_DIM)[None, :]).reshape(-1)


def kernel(x, norm_g, ffn_w_gate, ffn_w_up, ffn_w_down, ev_w_in, ev_w_out, ev_conv_w, ev_conv_b,
           ev_ln_g, ev_ln_b, ev_q_norm_g, ev_k_norm_g, ev_sinks, od_w_in, od_w_out, od_q_norm_g,
           od_k_norm_g, od_pool_w, od_pool_scale):
    bd = _block_diag_ones()
    ev_bias = jnp.asarray(_band_bias(1, SWA_WINDOW - 1))
    od_bias = jnp.asarray(np.stack([_band_bias(1, BLOCK), _band_bias(4, BLOCK),
                                    _band_bias(16, BLOCK)]))
    to_res16 = _tile_permutation()
    from_res16 = to_res16.T
    ffn_w = (ffn_w_gate.astype(BF16), ffn_w_up.astype(BF16), ffn_w_down.astype(BF16))
    x = x.reshape(BATCH * SEQ, D_MODEL)
    for layer in range(DEPTH):
        g = norm_g[layer]
        i = layer // 2
        x = _ffn_call(x, g[0:1], ffn_w, layer, 0)
        if layer % 2 == 0:
            w_in = jnp.concatenate([ev_w_in[i][:, :1024], ev_w_in[i][:, 1024 + _EV_Q_PERM],
                                    ev_w_in[i][:, 1536:]], axis=1).astype(BF16)
            u, q, k, v = _even_in_call(x, g[1:2], w_in, bd,
                                       _tile_heads(ev_q_norm_g[i], 8), _tile_heads(ev_k_norm_g[i], 2))
            sink = ev_sinks[i][_EV_HEAD_ORDER]
            r3 = lambda t: t.reshape(BATCH, SEQ, t.shape[-1])
            shift = _qk_shift(ev_q_norm_g[i], ev_k_norm_g[i])
            use_shift = jnp.logical_and(shift <= MAX_CONST_SHIFT,
                                        jnp.max(sink) - shift <= MAX_SINK_OVER_SHIFT)
            sink_terms = jnp.repeat(jnp.exp(sink - shift), HEAD_DIM).reshape(SWA_Q_HEADS // 2, 1, LANES)
            mix = lambda sk, bs, exact: _even_mix_call(r3(q), r3(k), r3(v), sk, bs, exact)
            fb = lax.cond(use_shift,
                          lambda: mix(sink_terms, ev_bias - shift, False),
                          lambda: mix(sink.reshape(SWA_Q_HEADS, 1, 1), ev_bias, True))
            w_out = ev_w_out[i].astype(BF16)
            extra = (u, jnp.pad(ev_conv_w[i], ((0, 1), (0, 0))), ev_conv_b[i][None],
                     ev_ln_g[i][None], ev_ln_b[i][None], fb.reshape(-1, 512),
                     w_out[:512], w_out[512 + _EV_Q_PERM])
            x = _ffn_call(x, g[2:3], ffn_w, layer, 1, "even", extra)
        else:
            q, k, v, u = _odd_in_call(x, g[1:2], to_res16, od_w_in[i].astype(BF16), bd,
                                      _tile_heads(od_q_norm_g[i], 8), _tile_heads(od_k_norm_g[i], 8))
            shift = _qk_shift(od_q_norm_g[i], od_k_norm_g[i])
            att = lax.cond(shift <= MAX_CONST_SHIFT,
                           lambda: _dil_attn_call(q, k, v, od_bias - shift, False),
                           lambda: _dil_attn_call(q, k, v, od_bias, True))
            pool = _pool_call(u, od_pool_w[i].astype(BF16), od_pool_scale[i][None])
            w_out = od_w_out[i].astype(BF16)
            extra = (att, pool, from_res16, w_out[:512], w_out[512:])
            x = _ffn_call(x, g[2:3], ffn_w, layer, 1, "odd", extra)
    return x.reshape(BATCH, SEQ, D_MODEL)
```
